```python
import math
import jax, jax.numpy as jnp
from jax import lax
import numpy as np

D_MODEL = 1024
BATCH = 4
SEQ = 8192
DEPTH = 2

N_A_LAYERS = DEPTH // 2
N_B_LAYERS = DEPTH - N_A_LAYERS
GLA_HEADS = 4
GLA_DK = D_MODEL // 2 // GLA_HEADS
GLA_DV = D_MODEL // GLA_HEADS
GLA_GATE_RANK = 16
GLA_GATE_TAU = 16.0
GLA_CHUNK = 64
GLA_IN = 2 * GLA_HEADS * GLA_DK + 2 * GLA_HEADS * GLA_DV + GLA_GATE_RANK
DIFF_HEADS = 8
DIFF_HD = D_MODEL // (2 * DIFF_HEADS)
DIFF_VD = 2 * DIFF_HD
Q_BLOCK = 128
ROPE_THETA = 10000.0
N_EXPERTS = 32
TOP_K = 4
D_FF = D_MODEL
SWIGLU_LIMIT = 7.0
SWIGLU_ALPHA = 1.702
EXPERT_BLOCK = 128
LN_EPS = 1e-5
RMS_EPS = 1e-5
DEEPNORM_ALPHA = (2.0 * DEPTH) ** 0.25
DEEPNORM_BETA = (8.0 * DEPTH) ** -0.25

kernel_name = "yoco_gla_diffattn_moe_deepnorm"


def layer_norm(x, g, b):
    xf = x.astype(jnp.float32)
    mu = jnp.mean(xf, axis=-1, keepdims=True)
    var = jnp.mean(jnp.square(xf - mu), axis=-1, keepdims=True)
    y = (xf - mu) * lax.rsqrt(var + LN_EPS)
    return (y * g.astype(jnp.float32) + b.astype(jnp.float32)).astype(x.dtype)


def rms_norm(x, w):
    xf = x.astype(jnp.float32)
    y = xf * lax.rsqrt(jnp.mean(jnp.square(xf), axis=-1, keepdims=True) + RMS_EPS)
    return (y * w.astype(jnp.float32)).astype(x.dtype)


def rope(t, positions):
    hd = t.shape[-1]
    half = hd // 2
    inv = ROPE_THETA ** (-jnp.arange(half, dtype=jnp.float32) * 2.0 / hd)
    ang = positions.astype(jnp.float32)[..., None] * inv
    cos = jnp.cos(ang)[:, :, None, :]
    sin = jnp.sin(ang)[:, :, None, :]
    t1 = t[..., :half].astype(jnp.float32)
    t2 = t[..., half:].astype(jnp.float32)
    return jnp.concatenate([t1 * cos - t2 * sin, t2 * cos + t1 * sin], axis=-1).astype(t.dtype)


def gla_mixer(x, w_in, w_gk2, b_gk, norm_w, w_out):
    B, S, D = x.shape
    H, DK, DV, C = GLA_HEADS, GLA_DK, GLA_DV, GLA_CHUNK
    N = S // C
    proj = x @ w_in
    q, k, v, r, gk_low = jnp.split(
        proj, [H * DK, 2 * H * DK, 2 * H * DK + H * DV, 2 * H * DK + 2 * H * DV], axis=-1)
    log_g = jax.nn.log_sigmoid((gk_low @ w_gk2 + b_gk).astype(jnp.float32)) / GLA_GATE_TAU

    def to_chunks(t, d):
        return t.astype(jnp.float32).reshape(B, N, C, H, d).transpose(0, 3, 1, 2, 4)

    qc = to_chunks(q, DK) * (DK ** -0.5)
    kc = to_chunks(k, DK)
    vc = to_chunks(v, DV)
    G = jnp.cumsum(to_chunks(log_g, DK), axis=3)
    G_last = G[:, :, :, -1:, :]
    q_in = qc * jnp.exp(G)
    k_in = kc * jnp.exp(-G)
    causal = jnp.tril(jnp.ones((C, C), dtype=bool))
    scores = jnp.where(causal, jnp.einsum('bhnid,bhnjd->bhnij', q_in, k_in), 0.0)
    o_intra = jnp.einsum('bhnij,bhnjv->bhniv', scores, vc)
    k_state = kc * jnp.exp(G_last - G)
    decay = jnp.exp(G_last[:, :, :, 0, :])

    def step(state, inp):
        q_c, k_c, v_c, d_c = inp
        o = jnp.einsum('bhcd,bhdv->bhcv', q_c, state)
        state = state * d_c[..., None] + jnp.einsum('bhcd,bhcv->bhdv', k_c, v_c)
        return state, o

    xs = (q_in.transpose(2, 0, 1, 3, 4), k_state.transpose(2, 0, 1, 3, 4),
          vc.transpose(2, 0, 1, 3, 4), decay.transpose(2, 0, 1, 3))
    state0 = jnp.zeros((B, H, DK, DV), jnp.float32)
    _, o_inter = lax.scan(step, state0, xs)
    o = o_intra + o_inter.transpose(1, 2, 0, 3, 4)
    o = o.transpose(0, 2, 3, 1, 4).reshape(B, S, H, DV).astype(x.dtype)
    o = rms_norm(o, norm_w) * jax.nn.silu(r.reshape(B, S, H, DV))
    return o.reshape(B, S, H * DV) @ w_out


def shared_kv(x, kv_w, positions):
    B, S, D = x.shape
    H, HD, VD = DIFF_HEADS, DIFF_HD, DIFF_VD
    kv = x @ kv_w
    k, v = jnp.split(kv, [2 * H * HD], axis=-1)
    k = rope(k.reshape(B, S, 2 * H, HD), positions).reshape(B, S, H, 2, HD)
    v = v.reshape(B, S, H, VD)
    return k, v


def diff_mixer(x, positions, k_sh, v_sh, w_q, lam, subln_w, w_out, lambda_init):
    B, S, D = x.shape
    H, HD, VD = DIFF_HEADS, DIFF_HD, DIFF_VD
    q = rope((x @ w_q).reshape(B, S, 2 * H, HD), positions).reshape(B, S, H, 2, HD) * (HD ** -0.5)
    lamf = lam.astype(jnp.float32)
    lam_full = (jnp.exp(jnp.sum(lamf[0] * lamf[1])) - jnp.exp(jnp.sum(lamf[2] * lamf[3]))
                + lambda_init)
    n_blk = S // Q_BLOCK
    q_blocks = q.reshape(B, n_blk, Q_BLOCK, H, 2, HD).transpose(1, 0, 2, 3, 4, 5)
    key_pos = jnp.arange(S, dtype=jnp.int32)

    def attend(args):
        q_b, blk = args
        s = jnp.einsum('bqhcd,bkhcd->bhcqk', q_b, k_sh).astype(jnp.float32)
        q_pos = blk * Q_BLOCK + jnp.arange(Q_BLOCK, dtype=jnp.int32)
        mask = key_pos[None, :] <= q_pos[:, None]
        p = jax.nn.softmax(jnp.where(mask, s, -jnp.inf), axis=-1)
        a = p[:, :, 0] - lam_full * p[:, :, 1]
        return jnp.einsum('bhqk,bkhv->bqhv', a.astype(v_sh.dtype), v_sh)

    o = lax.map(attend, (q_blocks, jnp.arange(n_blk, dtype=jnp.int32)))
    o = o.transpose(1, 0, 2, 3, 4).reshape(B, S, H, VD)
    o = rms_norm(o, subln_w) * (1.0 - lambda_init)
    return o.reshape(B, S, H * VD) @ w_out


def moe_ffn(x, router_w, router_b, w_gu, b_gu, w_down, b_down):
    B, S, D = x.shape
    T = B * S
    F = D_FF
    xt = x.reshape(T, D)
    logits = (xt @ router_w + router_b).astype(jnp.float32)
    top_vals, top_idx = lax.top_k(logits, TOP_K)
    gates = jax.nn.softmax(top_vals, axis=-1)
    n_assign = T * TOP_K
    flat_e = top_idx.reshape(n_assign).astype(jnp.int32)
    flat_tok = jnp.repeat(jnp.arange(T, dtype=jnp.int32), TOP_K)
    flat_g = gates.reshape(n_assign)
    order = jnp.argsort(flat_e, stable=True)
    se, stok, sg = flat_e[order], flat_tok[order], flat_g[order]
    counts = jnp.bincount(flat_e, length=N_EXPERTS).astype(jnp.int32)
    padded = (counts + EXPERT_BLOCK - 1) // EXPERT_BLOCK * EXPERT_BLOCK
    start = jnp.cumsum(counts) - counts
    pad_end = jnp.cumsum(padded)
    pad_start = pad_end - padded
    dest = pad_start[se] + jnp.arange(n_assign, dtype=jnp.int32) - start[se]
    n_blocks = -(-n_assign // EXPERT_BLOCK) + N_EXPERTS
    n_pad = n_blocks * EXPERT_BLOCK
    buf_tok = jnp.zeros((n_pad,), jnp.int32).at[dest].set(stok)
    buf_g = jnp.zeros((n_pad,), jnp.float32).at[dest].set(sg)
    blk_start = jnp.arange(n_blocks, dtype=jnp.int32) * EXPERT_BLOCK
    blk_e = jnp.minimum(jnp.searchsorted(pad_end, blk_start, side='right'), N_EXPERTS - 1)
    xb = xt[buf_tok].reshape(n_blocks, EXPERT_BLOCK, D)

    def expert_block(args):
        xblk, e = args
        h = xblk @ w_gu[e] + b_gu[e]
        x_glu = jnp.minimum(h[:, :F], SWIGLU_LIMIT)
        x_lin = jnp.clip(h[:, F:], -SWIGLU_LIMIT, SWIGLU_LIMIT)
        act = x_glu * jax.nn.sigmoid(SWIGLU_ALPHA * x_glu) * (x_lin + 1.0)
        return act @ w_down[e] + b_down[e]

    yb = lax.map(expert_block, (xb, blk_e))
    contrib = (yb.reshape(n_pad, D) * buf_g[:, None]).astype(x.dtype)
    y = jnp.zeros((T, D), x.dtype).at[buf_tok].add(contrib)
    return y.reshape(B, S, D)


def setup_inputs(seed: int = 0) -> dict:
    key = jax.random.key(seed)
    ks = jax.random.split(key, 32)
    D, H, DK, DV = D_MODEL, GLA_HEADS, GLA_DK, GLA_DV
    HB, HD, VD = DIFF_HEADS, DIFF_HD, DIFF_VD
    E, F = N_EXPERTS, D_FF
    beta = DEEPNORM_BETA
    nrm = jax.random.normal
    x = nrm(ks[0], (BATCH, SEQ, D), jnp.float32)
    offset = jax.random.randint(ks[1], (BATCH, 1), 0, 1024, dtype=jnp.int32)
    positions = (jnp.arange(SEQ, dtype=jnp.int32)[None, :] + offset).astype(jnp.int32)
    sD = D ** -0.5
    a_qk = nrm(ks[2], (N_A_LAYERS, D, 2 * H * DK), jnp.float32) * sD
    a_v = nrm(ks[3], (N_A_LAYERS, D, H * DV), jnp.float32) * sD * beta
    a_rg = nrm(ks[4], (N_A_LAYERS, D, H * DV + GLA_GATE_RANK), jnp.float32) * sD
    a_w_in = jnp.concatenate([a_qk, a_v, a_rg], axis=-1)
    a_w_gk2 = nrm(ks[5], (N_A_LAYERS, GLA_GATE_RANK, H * DK), jnp.float32) * GLA_GATE_RANK ** -0.5
    a_b_gk = 0.1 * nrm(ks[6], (N_A_LAYERS, H * DK), jnp.float32)
    a_norm_w = 1.0 + 0.02 * nrm(ks[7], (N_A_LAYERS, DV), jnp.float32)
    a_w_out = nrm(ks[8], (N_A_LAYERS, H * DV, D), jnp.float32) * (H * DV) ** -0.5 * beta
    kv_k = nrm(ks[9], (D, 2 * HB * HD), jnp.float32) * sD
    kv_v = nrm(ks[10], (D, HB * VD), jnp.float32) * sD * beta
    kv_w = jnp.concatenate([kv_k, kv_v], axis=-1)
    b_w_q = nrm(ks[11], (N_B_LAYERS, D, 2 * HB * HD), jnp.float32) * sD
    b_lambda = 0.1 * nrm(ks[12], (N_B_LAYERS, 4, HD), jnp.float32)
    b_subln_w = 1.0 + 0.02 * nrm(ks[13], (N_B_LAYERS, VD), jnp.float32)
    b_w_out = nrm(ks[14], (N_B_LAYERS, HB * VD, D), jnp.float32) * (HB * VD) ** -0.5 * beta
    ln1_g = 1.0 + 0.02 * nrm(ks[15], (DEPTH, D), jnp.float32)
    ln1_b = 0.02 * nrm(ks[16], (DEPTH, D), jnp.float32)
    ln2_g = 1.0 + 0.02 * nrm(ks[17], (DEPTH, D), jnp.float32)
    ln2_b = 0.02 * nrm(ks[18], (DEPTH, D), jnp.float32)
    router_w = nrm(ks[19], (DEPTH, D, E), jnp.float32) * sD
    router_b = 0.01 * nrm(ks[20], (DEPTH, E), jnp.float32)
    moe_w_gu = nrm(ks[21], (DEPTH, E, D, 2 * F), jnp.float32) * sD
    moe_b_gu = 0.01 * nrm(ks[22], (DEPTH, E, 2 * F), jnp.float32)
    moe_w_down = nrm(ks[23], (DEPTH, E, F, D), jnp.float32) * F ** -0.5 * beta
    moe_b_down = 0.01 * nrm(ks[24], (DEPTH, E, D), jnp.float32)
    return {"x": x, "positions": positions,
            "a_w_in": a_w_in, "a_w_gk2": a_w_gk2, "a_b_gk": a_b_gk, "a_norm_w": a_norm_w, "a_w_out": a_w_out,
            "kv_w": kv_w, "b_w_q": b_w_q, "b_lambda": b_lambda, "b_subln_w": b_subln_w, "b_w_out": b_w_out,
            "ln1_g": ln1_g, "ln1_b": ln1_b, "ln2_g": ln2_g, "ln2_b": ln2_b,
            "router_w": router_w, "router_b": router_b,
            "moe_w_gu": moe_w_gu, "moe_b_gu": moe_b_gu, "moe_w_down": moe_w_down, "moe_b_down": moe_b_down}


def reference(x, positions, a_w_in, a_w_gk2, a_b_gk, a_norm_w, a_w_out, kv_w, b_w_q, b_lambda, b_subln_w,
              b_w_out, ln1_g, ln1_b, ln2_g, ln2_b, router_w, router_b, moe_w_gu, moe_b_gu, moe_w_down,
              moe_b_down):
    k_sh, v_sh = None, None
    if N_A_LAYERS == 0:
        k_sh, v_sh = shared_kv(x, kv_w, positions)
    for l in range(DEPTH):
        if l < N_A_LAYERS:
            h = gla_mixer(x, a_w_in[l], a_w_gk2[l], a_b_gk[l], a_norm_w[l], a_w_out[l])
        else:
            j = l - N_A_LAYERS
            lambda_init = 0.8 - 0.6 * math.exp(-0.3 * l)
            h = diff_mixer(x, positions, k_sh, v_sh, b_w_q[j], b_lambda[j], b_subln_w[j], b_w_out[j],
                           lambda_init)
        x = layer_norm(DEEPNORM_ALPHA * x + h, ln1_g[l], ln1_b[l])
        m = moe_ffn(x, router_w[l], router_b[l], moe_w_gu[l], moe_b_gu[l], moe_w_down[l], moe_b_down[l])
        x = layer_norm(DEEPNORM_ALPHA * x + m, ln2_g[l], ln2_b[l])
        if l == N_A_LAYERS - 1:
            k_sh, v_sh = shared_kv(x, kv_w, positions)
    return x
```

```python
import functools
import math

import jax
import jax.numpy as jnp
from jax import lax
from jax.experimental import pallas as pl
from jax.experimental.pallas import tpu as pltpu

F32 = jnp.float32
BF16 = jnp.bfloat16
HIGHEST = lax.Precision.HIGHEST

DEPTH = 2
N_A_LAYERS = DEPTH // 2
GLA_HEADS = 4
GLA_DK = 128
GLA_DV = 256
GLA_GATE_RANK = 16
GLA_GATE_TAU = 16.0
GLA_CHUNK = 64
DIFF_HEADS = 8
DIFF_HD = 64
DIFF_VD = 128
ROPE_THETA = 10000.0
N_EXPERTS = 32
TOP_K = 4
SWIGLU_LIMIT = 7.0
SWIGLU_ALPHA = 1.702
LN_EPS = 1e-5
RMS_EPS = 1e-5
DEEPNORM_ALPHA = (2.0 * DEPTH) ** 0.25

LANES = 128
VMEM_LIMIT_BYTES = 56 * 1024 * 1024

ROW_TILE = 512
EXPERT_ROWS = 256
MOVE_ROWS = 256
ROUTE_ROWS = 512
ATTN_Q = 256
ATTN_K = 256
GLA_ROWS = 512


def _params(*sem):
    return pltpu.CompilerParams(dimension_semantics=sem, vmem_limit_bytes=VMEM_LIMIT_BYTES)


def _nt_dot(a, b):
    return lax.dot_general(a, b, (((1,), (1,)), ((), ())), preferred_element_type=F32)


def _gla_proj_kernel(x_ref, w_ref, wg_ref, wgk2_ref, bgk_ref, qkvr_ref, logg_ref):
    xb = x_ref[...].astype(BF16)
    n = w_ref.shape[1]
    for j in range(0, n, 512):
        qkvr_ref[:, j:j + 512] = jnp.dot(xb, w_ref[:, j:j + 512],
                                         preferred_element_type=F32).astype(BF16)
    gk_low = jnp.dot(xb, wg_ref[...], preferred_element_type=F32)
    z = jnp.dot(gk_low, wgk2_ref[...], preferred_element_type=F32, precision=HIGHEST) + bgk_ref[...]
    logg_ref[...] = (jnp.minimum(z, 0.0) - jnp.log1p(jnp.exp(-jnp.abs(z)))) / GLA_GATE_TAU


def _gla_proj(x2d, w_main, w_gate, w_gk2, b_gk):
    t, d = x2d.shape
    n = w_main.shape[1]
    hk = w_gk2.shape[1]
    tm = min(ROW_TILE, t)
    return pl.pallas_call(
        _gla_proj_kernel,
        grid=(t // tm,),
        in_specs=[
            pl.BlockSpec((tm, d), lambda i: (i, 0)),
            pl.BlockSpec((d, n), lambda i: (0, 0)),
            pl.BlockSpec((d, LANES), lambda i: (0, 0)),
            pl.BlockSpec((LANES, hk), lambda i: (0, 0)),
            pl.BlockSpec((1, hk), lambda i: (0, 0)),
        ],
        out_specs=[
            pl.BlockSpec((tm, n), lambda i: (i, 0)),
            pl.BlockSpec((tm, hk), lambda i: (i, 0)),
        ],
        out_shape=[jax.ShapeDtypeStruct((t, n), BF16), jax.ShapeDtypeStruct((t, hk), F32)],
        compiler_params=_params("arbitrary"),
    )(x2d, w_main, w_gate, w_gk2, b_gk)


def _gla_kernel(q_ref, k_ref, v_ref, r_ref, lg_ref, nw_ref, o_ref, st_ref):
    c = GLA_CHUNK

    @pl.when(pl.program_id(2) == 0)
    def _():
        st_ref[...] = jnp.zeros_like(st_ref)

    row = lax.broadcasted_iota(jnp.int32, (c, c), 0)
    col = lax.broadcasted_iota(jnp.int32, (c, c), 1)
    tril = row >= col
    tril_b = jnp.where(tril, 1.0, 0.0).astype(BF16)
    n_chunks = q_ref.shape[0] // c

    def chunk(ci, carry):
        r0 = pl.multiple_of(ci * c, c)
        lg = lg_ref[pl.ds(r0, c), :]
        hi = lg.astype(BF16)
        rem = lg - hi.astype(F32)
        mid = rem.astype(BF16)
        lo = (rem - mid.astype(F32)).astype(BF16)
        g = (jnp.dot(tril_b, hi, preferred_element_type=F32)
             + jnp.dot(tril_b, mid, preferred_element_type=F32)
             + jnp.dot(tril_b, lo, preferred_element_type=F32))
        g_last = g[c - 1:c, :]
        q = q_ref[pl.ds(r0, c), :].astype(F32) * (GLA_DK ** -0.5)
        k = k_ref[pl.ds(r0, c), :].astype(F32)
        v = v_ref[pl.ds(r0, c), :]
        q_in = (q * jnp.exp(g)).astype(BF16)
        k_in = (k * jnp.exp(-g)).astype(BF16)
        k_state = (k * jnp.exp(g_last - g)).astype(BF16)
        decay = jnp.exp(g_last)
        scores = jnp.where(tril, _nt_dot(q_in, k_in), 0.0).astype(BF16)
        st = st_ref[...]
        o = jnp.dot(scores, v, preferred_element_type=F32) + _nt_dot(q_in, st.astype(BF16))
        v_t = v.astype(F32).T.astype(BF16)
        st_ref[...] = st * decay + jnp.dot(v_t, k_state, preferred_element_type=F32)
        o = o * lax.rsqrt(jnp.mean(o * o, axis=-1, keepdims=True) + RMS_EPS) * nw_ref[...]
        r = r_ref[pl.ds(r0, c), :].astype(F32)
        o_ref[pl.ds(r0, c), :] = (o * (r * jax.nn.sigmoid(r))).astype(BF16)
        return carry

    lax.fori_loop(0, n_chunks, chunk, 0)


def _gla_core(qkvr, logg, norm_w, b, s):
    t = b * s
    h, dk, dv = GLA_HEADS, GLA_DK, GLA_DV
    ts = min(GLA_ROWS, s)
    ns = s // ts
    k_off = (h * dk) // dk
    v_off = (2 * h * dk) // dv
    r_off = (2 * h * dk + h * dv) // dv
    return pl.pallas_call(
        _gla_kernel,
        grid=(b, h, ns),
        in_specs=[
            pl.BlockSpec((ts, dk), lambda bi, hi, si: (bi * ns + si, hi)),
            pl.BlockSpec((ts, dk), lambda bi, hi, si: (bi * ns + si, k_off + hi)),
            pl.BlockSpec((ts, dv), lambda bi, hi, si: (bi * ns + si, v_off + hi)),
            pl.BlockSpec((ts, dv), lambda bi, hi, si: (bi * ns + si, r_off + hi)),
            pl.BlockSpec((ts, dk), lambda bi, hi, si: (bi * ns + si, hi)),
            pl.BlockSpec((1, dv), lambda bi, hi, si: (0, 0)),
        ],
        out_specs=pl.BlockSpec((ts, dv), lambda bi, hi, si: (bi * ns + si, hi)),
        out_shape=jax.ShapeDtypeStruct((t, h * dv), BF16),
        scratch_shapes=[pltpu.VMEM((dv, dk), F32)],
        compiler_params=_params("arbitrary", "arbitrary", "arbitrary"),
    )(qkvr, qkvr, qkvr, qkvr, logg, norm_w)


def _layer_norm(y, g, b):
    mu = jnp.mean(y, axis=-1, keepdims=True)
    yc = y - mu
    var = jnp.mean(yc * yc, axis=-1, keepdims=True)
    return yc * lax.rsqrt(var + LN_EPS) * g + b


def _to_row_tiles(ref, val):
    for j in range(ref.shape[1]):
        ref[:, j, :] = val[:, j * LANES:(j + 1) * LANES]


def _proj_ln_kernel(o_ref, w_ref, x_ref, g_ref, b_ref, rw_ref, rb_ref, x1_ref, x1r_ref, logit_ref):
    h = jnp.dot(o_ref[...], w_ref[...], preferred_element_type=F32)
    x1 = _layer_norm(DEEPNORM_ALPHA * x_ref[...] + h, g_ref[...], b_ref[...])
    x1_ref[...] = x1
    _to_row_tiles(x1r_ref, x1)
    logit_ref[...] = jnp.dot(x1, rw_ref[...], preferred_element_type=F32, precision=HIGHEST) + rb_ref[...]


def _proj_ln(o, w_out, x2d, ln_g, ln_b, rw, rb):
    t, d = x2d.shape
    kdim = o.shape[1]
    tm = min(ROW_TILE, t)
    return pl.pallas_call(
        _proj_ln_kernel,
        grid=(t // tm,),
        in_specs=[
            pl.BlockSpec((tm, kdim), lambda i: (i, 0)),
            pl.BlockSpec((kdim, d), lambda i: (0, 0)),
            pl.BlockSpec((tm, d), lambda i: (i, 0)),
            pl.BlockSpec((1, d), lambda i: (0, 0)),
            pl.BlockSpec((1, d), lambda i: (0, 0)),
            pl.BlockSpec((d, LANES), lambda i: (0, 0)),
            pl.BlockSpec((1, LANES), lambda i: (0, 0)),
        ],
        out_specs=[
            pl.BlockSpec((tm, d), lambda i: (i, 0)),
            pl.BlockSpec((tm, d // LANES, LANES), lambda i: (i, 0, 0)),
            pl.BlockSpec((tm, LANES), lambda i: (i, 0)),
        ],
        out_shape=[jax.ShapeDtypeStruct((t, d), F32), jax.ShapeDtypeStruct((t, d // LANES, LANES), F32),
                   jax.ShapeDtypeStruct((t, LANES), F32)],
        compiler_params=_params("arbitrary"),
    )(o, w_out, x2d, ln_g, ln_b, rw, rb)


def _route_kernel(lt_ref, ei_ref, gate_ref, cnt_ref, carry_ref):
    e, tb = lt_ref.shape

    @pl.when(pl.program_id(0) == 0)
    def _():
        carry_ref[...] = jnp.zeros_like(carry_ref)

    vals = lt_ref[...]
    eidx = lax.broadcasted_iota(jnp.int32, (e, tb), 0).astype(F32)
    top_v, top_i, hots = [], [], []
    for _ in range(TOP_K):
        m = jnp.max(vals, axis=0, keepdims=True)
        idx = jnp.min(jnp.where(vals == m, eidx, float(e)), axis=0, keepdims=True)
        hot = eidx == idx
        vals = jnp.where(hot, -jnp.inf, vals)
        top_v.append(m)
        top_i.append(idx)
        hots.append(hot)
    ex = [jnp.exp(v - top_v[0]) for v in top_v]
    den = ex[0] + ex[1] + ex[2] + ex[3]
    gates = [x / den for x in ex]
    multi = jnp.zeros((e, tb), F32)
    for hot in hots:
        multi = multi + jnp.where(hot, 1.0, 0.0)
    ri = lax.broadcasted_iota(jnp.int32, (tb, tb), 0)
    ci = lax.broadcasted_iota(jnp.int32, (tb, tb), 1)
    upper = jnp.where(ri <= ci, 1.0, 0.0).astype(BF16)
    incl = jnp.dot(multi.astype(BF16), upper, preferred_element_type=F32)
    before = incl - multi + carry_ref[:, 0:1]
    ranks = [jnp.sum(jnp.where(hot, before, 0.0), axis=0, keepdims=True) for hot in hots]
    ei_ref[...] = jnp.concatenate(top_i + ranks, axis=0).astype(jnp.int32)
    gate_ref[...] = jnp.concatenate(gates + [jnp.zeros_like(g) for g in gates], axis=0)
    new_carry = carry_ref[...] + incl[:, tb - 1:tb]
    carry_ref[...] = new_carry
    cnt_ref[...] = new_carry


def _route(logits_t):
    e, t = logits_t.shape
    tb = min(ROUTE_ROWS, t)
    return pl.pallas_call(
        _route_kernel,
        grid=(t // tb,),
        in_specs=[pl.BlockSpec((e, tb), lambda i: (0, i))],
        out_specs=[
            pl.BlockSpec((2 * TOP_K, tb), lambda i: (0, i)),
            pl.BlockSpec((2 * TOP_K, tb), lambda i: (0, i)),
            pl.BlockSpec((e, LANES), lambda i: (0, 0)),
        ],
        out_shape=[
            jax.ShapeDtypeStruct((2 * TOP_K, t), jnp.int32),
            jax.ShapeDtypeStruct((2 * TOP_K, t), F32),
            jax.ShapeDtypeStruct((e, LANES), F32),
        ],
        scratch_shapes=[pltpu.VMEM((e, LANES), F32)],
        compiler_params=_params("arbitrary"),
    )(logits_t)


def _row_copy(src, src_row, dst, dst_row, sem):
    return pltpu.make_async_copy(src.at[src_row], dst.at[dst_row], sem)


def _scatter_kernel(tail_ref, x_ref, dest_hbm, xb_hbm, dest_smem, zero_ref, sem, idx_sem):
    i = pl.program_id(0)
    tb = x_ref.shape[0]
    bm = zero_ref.shape[0]

    idx_cp = pltpu.make_async_copy(dest_hbm.at[i], dest_smem, idx_sem)
    idx_cp.start()

    @pl.when(i == 0)
    def _():
        zero_ref[...] = jnp.zeros_like(zero_ref)
        n_blocks = xb_hbm.shape[0] // bm
        n_used = tail_ref[N_EXPERTS]

        def zero_copy(row):
            return pltpu.make_async_copy(zero_ref, xb_hbm.at[pl.ds(row, bm)], sem)

        def start(e, c):
            @pl.when(tail_ref[e] >= 0)
            def _():
                zero_copy(tail_ref[e]).start()
            return c

        def wait(e, c):
            @pl.when(tail_ref[e] >= 0)
            def _():
                zero_copy(tail_ref[e]).wait()
            return c

        def start_unused(blk, c):
            zero_copy(blk * bm).start()
            return c

        def wait_unused(blk, c):
            zero_copy(blk * bm).wait()
            return c

        lax.fori_loop(0, N_EXPERTS, start, 0)
        lax.fori_loop(n_used, n_blocks, start_unused, 0)
        lax.fori_loop(0, N_EXPERTS, wait, 0)
        lax.fori_loop(n_used, n_blocks, wait_unused, 0)

    idx_cp.wait()

    def start_row(j, c):
        _row_copy(x_ref, j % tb, xb_hbm, dest_smem[j], sem).start()
        return c

    def wait_row(j, c):
        _row_copy(x_ref, j % tb, xb_hbm, dest_smem[j], sem).wait()
        return c

    lax.fori_loop(0, TOP_K * tb, start_row, 0)
    lax.fori_loop(0, TOP_K * tb, wait_row, 0)


def _scatter_rows(x1r, dest_blocks, tails, n_pad):
    t, sub, lanes = x1r.shape
    nblk, per = dest_blocks.shape
    tb = per // TOP_K
    return pl.pallas_call(
        _scatter_kernel,
        grid_spec=pltpu.PrefetchScalarGridSpec(
            num_scalar_prefetch=1,
            grid=(nblk,),
            in_specs=[
                pl.BlockSpec((tb, sub, lanes), lambda i, tl: (i, 0, 0)),
                pl.BlockSpec(memory_space=pl.ANY),
            ],
            out_specs=pl.BlockSpec(memory_space=pl.ANY),
            scratch_shapes=[
                pltpu.SMEM((per,), jnp.int32),
                pltpu.VMEM((EXPERT_ROWS, sub, lanes), F32),
                pltpu.SemaphoreType.DMA,
                pltpu.SemaphoreType.DMA,
            ],
        ),
        out_shape=jax.ShapeDtypeStruct((n_pad, sub, lanes), F32),
        compiler_params=_params("arbitrary"),
    )(tails, x1r, dest_blocks)


def _expert_kernel(be_ref, nu_ref, x_ref, wgu_ref, bgu_ref, wd_ref, bd_ref, y_ref, wgu_b, wd_b, xs_ref):
    i = pl.program_id(0)
    f = wd_ref.shape[0]

    @pl.when(i >= nu_ref[0])
    def _():
        y_ref[...] = jnp.zeros_like(y_ref)

    @pl.when(i < nu_ref[0])
    def _():
        prev = be_ref[jnp.maximum(i - 1, 0)]

        @pl.when((i == 0) | (be_ref[i] != prev))
        def _():
            wgu_b[...] = wgu_ref[...].astype(BF16)
            wd_b[...] = wd_ref[...].astype(BF16)

        for j in range(x_ref.shape[1]):
            xs_ref[:, j * LANES:(j + 1) * LANES] = x_ref[:, j, :].astype(BF16)
        xb = xs_ref[...]
        fc = 512
        acc = jnp.zeros(xs_ref.shape, F32) + bd_ref[...]
        for j in range(0, f, fc):
            h_glu = jnp.dot(xb, wgu_b[:, j:j + fc], preferred_element_type=F32) + bgu_ref[:, j:j + fc]
            h_lin = (jnp.dot(xb, wgu_b[:, f + j:f + j + fc], preferred_element_type=F32)
                     + bgu_ref[:, f + j:f + j + fc])
            x_glu = jnp.minimum(h_glu, SWIGLU_LIMIT)
            x_lin = jnp.clip(h_lin, -SWIGLU_LIMIT, SWIGLU_LIMIT)
            act = x_glu * jax.nn.sigmoid(SWIGLU_ALPHA * x_glu) * (x_lin + 1.0)
            acc = acc + jnp.dot(act.astype(BF16), wd_b[j:j + fc, :], preferred_element_type=F32)
        _to_row_tiles(y_ref, acc)


def _expert_ffn(xb, blk_e, n_used, w_gu, b_gu, w_down, b_down, layer):
    n_pad, sub, lanes = xb.shape
    d = sub * lanes
    _, e, _, f2 = w_gu.shape
    f = f2 // 2
    bm = EXPERT_ROWS
    nb = n_pad // bm
    b_gu4 = b_gu.reshape(b_gu.shape[0], e, 1, f2)
    b_down4 = b_down.reshape(b_down.shape[0], e, 1, d)

    def row_map(i, be, nu):
        return (jnp.minimum(i, nu[0] - 1), 0, 0)

    def w_map(i, be, nu):
        return (layer, be[i], 0, 0)

    return pl.pallas_call(
        _expert_kernel,
        grid_spec=pltpu.PrefetchScalarGridSpec(
            num_scalar_prefetch=2,
            grid=(nb,),
            in_specs=[
                pl.BlockSpec((bm, sub, lanes), row_map),
                pl.BlockSpec((None, None, d, f2), w_map),
                pl.BlockSpec((None, None, 1, f2), w_map),
                pl.BlockSpec((None, None, f, d), w_map),
                pl.BlockSpec((None, None, 1, d), w_map),
            ],
            out_specs=pl.BlockSpec((bm, sub, lanes), lambda i, be, nu: (i, 0, 0)),
            scratch_shapes=[pltpu.VMEM((d, f2), BF16), pltpu.VMEM((f, d), BF16), pltpu.VMEM((bm, d), BF16)],
        ),
        out_shape=jax.ShapeDtypeStruct((n_pad, sub, lanes), F32),
        compiler_params=_params("arbitrary"),
    )(blk_e, n_used, xb, w_gu, b_gu4, w_down, b_down4)


def _combine_kernel(x_ref, gate_ref, g_ref, b_ref, dest_hbm, yb_hbm, o_ref, dest_smem, buf, sem, idx_sem):
    i = pl.program_id(0)
    tb = x_ref.shape[0]

    idx_cp = pltpu.make_async_copy(dest_hbm.at[i], dest_smem, idx_sem)
    idx_cp.start()
    idx_cp.wait()

    def start_row(j, c):
        _row_copy(yb_hbm, dest_smem[j], buf, j, sem).start()
        return c

    def wait_row(j, c):
        _row_copy(yb_hbm, dest_smem[j], buf, j, sem).wait()
        return c

    lax.fori_loop(0, TOP_K * tb, start_row, 0)
    lax.fori_loop(0, TOP_K * tb, wait_row, 0)

    cols = []
    for j in range(buf.shape[1]):
        mj = gate_ref[:, 0:1] * buf[0:tb, j, :]
        for k in range(1, TOP_K):
            mj = mj + gate_ref[:, k:k + 1] * buf[k * tb:(k + 1) * tb, j, :]
        cols.append(mj)
    m = jnp.concatenate(cols, axis=1)
    o_ref[...] = _layer_norm(DEEPNORM_ALPHA * x_ref[...] + m, g_ref[...], b_ref[...])


def _combine(x1, gates_t, ln_g, ln_b, dest_blocks, yb):
    t, d = x1.shape
    nblk, per = dest_blocks.shape
    tb = per // TOP_K
    return pl.pallas_call(
        _combine_kernel,
        grid=(nblk,),
        in_specs=[
            pl.BlockSpec((tb, d), lambda i: (i, 0)),
            pl.BlockSpec((tb, 2 * TOP_K), lambda i: (i, 0)),
            pl.BlockSpec((1, d), lambda i: (0, 0)),
            pl.BlockSpec((1, d), lambda i: (0, 0)),
            pl.BlockSpec(memory_space=pl.ANY),
            pl.BlockSpec(memory_space=pl.ANY),
        ],
        out_specs=pl.BlockSpec((tb, d), lambda i: (i, 0)),
        out_shape=jax.ShapeDtypeStruct((t, d), F32),
        scratch_shapes=[
            pltpu.SMEM((per,), jnp.int32),
            pltpu.VMEM((TOP_K * tb,) + yb.shape[1:], F32),
            pltpu.SemaphoreType.DMA,
            pltpu.SemaphoreType.DMA,
        ],
        compiler_params=_params("arbitrary"),
    )(x1, gates_t, ln_g, ln_b, dest_blocks, yb)


def _moe(x1, x1r, logits, ln_g, ln_b, w_gu, b_gu, w_down, b_down, layer):
    t, d = x1.shape
    e = N_EXPERTS
    bm = EXPERT_ROWS
    ei, gates, counts = _route(logits[:, :e].T)
    idx, rank = ei[:TOP_K], ei[TOP_K:]
    cnt = counts[:, 0].astype(jnp.int32)
    padded = (cnt + bm - 1) // bm * bm
    pad_end = jnp.cumsum(padded)
    pad_start = pad_end - padded
    onehot = idx[None] == jnp.arange(e, dtype=jnp.int32)[:, None, None]
    dest = rank + jnp.sum(jnp.where(onehot, pad_start[:, None, None], 0), axis=0)
    nb = (t * TOP_K) // bm + e
    n_pad = nb * bm
    n_used = (pad_end[-1] // bm).astype(jnp.int32)
    blk_start = jnp.arange(nb, dtype=jnp.int32) * bm
    blk_e = jnp.minimum(jnp.searchsorted(pad_end, blk_start, side="right"), e - 1).astype(jnp.int32)
    blk_e = jnp.where(jnp.arange(nb) < n_used, blk_e, blk_e[jnp.maximum(n_used - 1, 0)])
    tails = jnp.concatenate([jnp.where(padded > 0, pad_end - bm, -1), n_used.reshape(1)]).astype(jnp.int32)
    tb = min(MOVE_ROWS, t)
    dest_blocks = dest.reshape(TOP_K, t // tb, tb).transpose(1, 0, 2).reshape(t // tb, TOP_K * tb)
    xb = _scatter_rows(x1r, dest_blocks, tails, n_pad)
    yb = _expert_ffn(xb, blk_e, n_used.reshape(1), w_gu, b_gu, w_down, b_down, layer)
    return _combine(x1, gates.T, ln_g, ln_b, dest_blocks, yb)


def _qkv_rope_kernel(x_ref, w_ref, pos_ref, inv_ref, out_ref):
    xb = x_ref[...].astype(BF16)
    tm = x_ref.shape[0]
    n_rope = 2 * (2 * DIFF_HEADS * DIFF_HD)
    ang = pos_ref[...].astype(F32) * inv_ref[...]
    cos = jnp.cos(ang)
    sin = jnp.sin(ang)
    lane = lax.broadcasted_iota(jnp.int32, (tm, LANES), 1)
    first_half = (lane % DIFF_HD) < (DIFF_HD // 2)
    sin_signed = jnp.where(first_half, -sin, sin)
    q_scale = DIFF_HD ** -0.5
    n = w_ref.shape[1]
    for j in range(0, n, LANES):
        c = jnp.dot(xb, w_ref[:, j:j + LANES], preferred_element_type=F32)
        if j < n_rope:
            rot = jnp.where(first_half, pltpu.roll(c, LANES - DIFF_HD // 2, 1), pltpu.roll(c, DIFF_HD // 2, 1))
            c = c * cos + rot * sin_signed
            if j < n_rope // 2:
                c = c * q_scale
        out_ref[:, j:j + LANES] = c.astype(BF16)


def _qkv_rope(x2d, w_qkv, pos_col, inv_row):
    t, d = x2d.shape
    n = w_qkv.shape[1]
    tm = min(ROW_TILE, t)
    return pl.pallas_call(
        _qkv_rope_kernel,
        grid=(t // tm,),
        in_specs=[
            pl.BlockSpec((tm, d), lambda i: (i, 0)),
            pl.BlockSpec((d, n), lambda i: (0, 0)),
            pl.BlockSpec((tm, 1), lambda i: (i, 0)),
            pl.BlockSpec((1, LANES), lambda i: (0, 0)),
        ],
        out_specs=pl.BlockSpec((tm, n), lambda i: (i, 0)),
        out_shape=jax.ShapeDtypeStruct((t, n), BF16),
        compiler_params=_params("arbitrary"),
    )(x2d, w_qkv, pos_col, inv_row)


def _diff_attn_kernel(q_ref, k_ref, v_ref, lam_ref, sw_ref, o_ref, qs_ref, m_ref, l_ref, acc_ref, *, lambda_init):
    qi = pl.program_id(2)
    tq = q_ref.shape[0]
    tk = ATTN_K if k_ref.shape[0] >= ATTN_K else k_ref.shape[0]
    q = q_ref[...]
    lane = lax.broadcasted_iota(jnp.int32, q.shape, 1)
    zero = jnp.zeros_like(q)
    qs_ref[0:tq, :] = jnp.where(lane < DIFF_HD, q, zero)
    qs_ref[tq:2 * tq, :] = jnp.where(lane >= DIFF_HD, q, zero)
    m_ref[...] = jnp.full(m_ref.shape, -jnp.inf, F32)
    l_ref[...] = jnp.zeros_like(l_ref)
    acc_ref[...] = jnp.zeros_like(acc_ref)

    def block(kj, masked):
        k0 = pl.multiple_of(kj * tk, tk)
        s = _nt_dot(qs_ref[...], k_ref[pl.ds(k0, tk), :])
        if masked:
            qpos = qi * tq + lax.broadcasted_iota(jnp.int32, (2 * tq, tk), 0) % tq
            kpos = k0 + lax.broadcasted_iota(jnp.int32, (2 * tq, tk), 1)
            s = jnp.where(kpos <= qpos, s, -jnp.inf)
        m_old = m_ref[...]
        m_new = jnp.maximum(m_old, jnp.max(s, axis=-1, keepdims=True))
        p = jnp.exp(s - jnp.concatenate([m_new] * (tk // LANES), axis=1))
        alpha = jnp.exp(m_old - m_new)
        l_ref[...] = alpha * l_ref[...] + jnp.sum(p, axis=-1, keepdims=True)
        acc_ref[...] = alpha * acc_ref[...] + jnp.dot(p.astype(BF16), v_ref[pl.ds(k0, tk), :],
                                                      preferred_element_type=F32)
        m_ref[...] = m_new

    n_full = (qi * tq) // tk

    def body(kj, c):
        block(kj, False)
        return c

    lax.fori_loop(0, n_full, body, 0)
    for d in range(tq // tk):
        block(n_full + d, True)

    lam = lam_ref[...]
    lam_full = (jnp.exp(jnp.sum(lam[0:1] * lam[1:2], axis=-1, keepdims=True))
                - jnp.exp(jnp.sum(lam[2:3] * lam[3:4], axis=-1, keepdims=True)) + lambda_init)
    o1 = acc_ref[0:tq, :] / l_ref[0:tq, :]
    o2 = acc_ref[tq:2 * tq, :] / l_ref[tq:2 * tq, :]
    o = o1 - lam_full * o2
    o = o * lax.rsqrt(jnp.mean(o * o, axis=-1, keepdims=True) + RMS_EPS) * sw_ref[...]
    o_ref[...] = (o * (1.0 - lambda_init)).astype(BF16)


def _diff_attn(qkv, lam, subln_w, b, s, lambda_init):
    t = b * s
    h, w = DIFF_HEADS, 2 * DIFF_HD
    tq = min(ATTN_Q, s)
    nq = s // tq
    return pl.pallas_call(
        functools.partial(_diff_attn_kernel, lambda_init=lambda_init),
        grid=(b, h, nq),
        in_specs=[
            pl.BlockSpec((tq, w), lambda bi, hi, qi: (bi * nq + qi, hi)),
            pl.BlockSpec((s, w), lambda bi, hi, qi: (bi, h + hi)),
            pl.BlockSpec((s, DIFF_VD), lambda bi, hi, qi: (bi, 2 * h + hi)),
            pl.BlockSpec((4, DIFF_HD), lambda bi, hi, qi: (0, 0)),
            pl.BlockSpec((1, DIFF_VD), lambda bi, hi, qi: (0, 0)),
        ],
        out_specs=pl.BlockSpec((tq, DIFF_VD), lambda bi, hi, qi: (bi * nq + qi, hi)),
        out_shape=jax.ShapeDtypeStruct((t, h * DIFF_VD), BF16),
        scratch_shapes=[
            pltpu.VMEM((2 * tq, w), BF16),
            pltpu.VMEM((2 * tq, LANES), F32),
            pltpu.VMEM((2 * tq, LANES), F32),
            pltpu.VMEM((2 * tq, DIFF_VD), F32),
        ],
        compiler_params=_params("arbitrary", "arbitrary", "arbitrary"),
    )(qkv, qkv, qkv, lam, subln_w)


def _pad_cols(a, n):
    return jnp.pad(a, ((0, 0), (0, n - a.shape[1])))


def kernel(x, positions, a_w_in, a_w_gk2, a_b_gk, a_norm_w, a_w_out, kv_w, b_w_q, b_lambda, b_subln_w,
           b_w_out, ln1_g, ln1_b, ln2_g, ln2_b, router_w, router_b, moe_w_gu, moe_b_gu, moe_w_down,
           moe_b_down):
    b, s, d = x.shape
    t = b * s
    xc = x.reshape(t, d)
    n_main = 2 * GLA_HEADS * GLA_DK + 2 * GLA_HEADS * GLA_DV
    half = DIFF_HD // 2
    inv = ROPE_THETA ** (-jnp.arange(half, dtype=F32) * 2.0 / DIFF_HD)
    inv_row = jnp.tile(inv, LANES // half).reshape(1, LANES)
    pos_col = positions.reshape(t, 1)
    assert DEPTH - N_A_LAYERS == 1
    for l in range(DEPTH):
        if l < N_A_LAYERS:
            w_in = a_w_in[l]
            qkvr, logg = _gla_proj(
                xc, w_in[:, :n_main].astype(BF16), _pad_cols(w_in[:, n_main:], LANES).astype(BF16),
                jnp.pad(a_w_gk2[l], ((0, LANES - GLA_GATE_RANK), (0, 0))), a_b_gk[l].reshape(1, -1))
            o = _gla_core(qkvr, logg, a_norm_w[l].reshape(1, -1), b, s)
            w_out = a_w_out[l]
        else:
            j = l - N_A_LAYERS
            lambda_init = 0.8 - 0.6 * math.exp(-0.3 * l)
            w_qkv = jnp.concatenate([b_w_q[j], kv_w], axis=1).astype(BF16)
            qkv = _qkv_rope(xc, w_qkv, pos_col, inv_row)
            o = _diff_attn(qkv, b_lambda[j], b_subln_w[j].reshape(1, -1), b, s, lambda_init)
            w_out = b_w_out[j]
        x1, x1r, logits = _proj_ln(o, w_out.astype(BF16), xc, ln1_g[l].reshape(1, -1), ln1_b[l].reshape(1, -1),
                                   _pad_cols(router_w[l], LANES), _pad_cols(router_b[l].reshape(1, -1), LANES))
        xc = _moe(x1, x1r, logits, ln2_g[l].reshape(1, -1), ln2_b[l].reshape(1, -1),
                  moe_w_gu, moe_b_gu, moe_w_down, moe_b_down, l)
    return xc.reshape(b, s, d)
```

```python
import functools
import math

import jax
import jax.numpy as jnp
from jax import lax
from jax.experimental import pallas as pl
from jax.experimental.pallas import tpu as pltpu

F32 = jnp.float32
BF16 = jnp.bfloat16
HIGHEST = lax.Precision.HIGHEST

DEPTH = 2
N_A_LAYERS = DEPTH // 2
GLA_HEADS = 4
GLA_DK = 128
GLA_DV = 256
GLA_GATE_RANK = 16
GLA_GATE_TAU = 16.0
GLA_CHUNK = 64
DIFF_HEADS = 8
DIFF_HD = 64
DIFF_VD = 128
ROPE_THETA = 10000.0
N_EXPERTS = 32
TOP_K = 4
SWIGLU_LIMIT = 7.0
SWIGLU_ALPHA = 1.702
LN_EPS = 1e-5
RMS_EPS = 1e-5
DEEPNORM_ALPHA = (2.0 * DEPTH) ** 0.25

LANES = 128
SUBLANES = 8
BF16_EXACT_INT = 256
STRIP_ALIGN = SUBLANES
VMEM_LIMIT_BYTES = 56 * 1024 * 1024

ROW_TILE = 512
EXPERT_ROWS = 256
MOE_TOKENS = 512
ATTN_Q = 512
ATTN_K = 512
GLA_ROWS = 512


def _params(*sem):
    return pltpu.CompilerParams(dimension_semantics=sem, vmem_limit_bytes=VMEM_LIMIT_BYTES)


def _nt_dot(a, b):
    return lax.dot_general(a, b, (((1,), (1,)), ((), ())), preferred_element_type=F32)


def _gla_proj_kernel(x_ref, w_ref, wg_ref, wgk2_ref, bgk_ref, qkvr_ref, logg_ref):
    xb = x_ref[...].astype(BF16)
    n = w_ref.shape[1]
    for j in range(0, n, 512):
        qkvr_ref[:, j:j + 512] = jnp.dot(xb, w_ref[:, j:j + 512],
                                         preferred_element_type=F32).astype(BF16)
    gk_low = jnp.dot(xb, wg_ref[...], preferred_element_type=F32)
    z = jnp.dot(gk_low, wgk2_ref[...], preferred_element_type=F32, precision=HIGHEST) + bgk_ref[...]
    logg_ref[...] = (jnp.minimum(z, 0.0) - jnp.log1p(jnp.exp(-jnp.abs(z)))) / GLA_GATE_TAU


def _gla_proj(x2d, w_main, w_gate, w_gk2, b_gk):
    t, d = x2d.shape
    n = w_main.shape[1]
    hk = w_gk2.shape[1]
    tm = min(ROW_TILE, t)
    return pl.pallas_call(
        _gla_proj_kernel,
        grid=(t // tm,),
        in_specs=[
            pl.BlockSpec((tm, d), lambda i: (i, 0)),
            pl.BlockSpec((d, n), lambda i: (0, 0)),
            pl.BlockSpec((d, LANES), lambda i: (0, 0)),
            pl.BlockSpec((LANES, hk), lambda i: (0, 0)),
            pl.BlockSpec((1, hk), lambda i: (0, 0)),
        ],
        out_specs=[
            pl.BlockSpec((tm, n), lambda i: (i, 0)),
            pl.BlockSpec((tm, hk), lambda i: (i, 0)),
        ],
        out_shape=[jax.ShapeDtypeStruct((t, n), BF16), jax.ShapeDtypeStruct((t, hk), F32)],
        compiler_params=_params("arbitrary"),
        name="gla_proj",
    )(x2d, w_main, w_gate, w_gk2, b_gk)


def _gla_kernel(q_ref, k_ref, v_ref, r_ref, lg_ref, nw_ref, o_ref, st_ref):
    c = GLA_CHUNK

    @pl.when(pl.program_id(2) == 0)
    def _():
        st_ref[...] = jnp.zeros_like(st_ref)

    row = lax.broadcasted_iota(jnp.int32, (c, c), 0)
    col = lax.broadcasted_iota(jnp.int32, (c, c), 1)
    tril = row >= col
    tril_b = jnp.where(tril, 1.0, 0.0).astype(BF16)
    n_chunks = q_ref.shape[0] // c

    def chunk(ci, carry):
        r0 = pl.multiple_of(ci * c, c)
        lg = lg_ref[pl.ds(r0, c), :]
        hi = lg.astype(BF16)
        rem = lg - hi.astype(F32)
        mid = rem.astype(BF16)
        lo = (rem - mid.astype(F32)).astype(BF16)
        g = (jnp.dot(tril_b, hi, preferred_element_type=F32)
             + jnp.dot(tril_b, mid, preferred_element_type=F32)
             + jnp.dot(tril_b, lo, preferred_element_type=F32))
        g_last = g[c - 1:c, :]
        q = q_ref[pl.ds(r0, c), :].astype(F32) * (GLA_DK ** -0.5)
        k = k_ref[pl.ds(r0, c), :].astype(F32)
        v = v_ref[pl.ds(r0, c), :]
        q_in = (q * jnp.exp(g)).astype(BF16)
        k_in = (k * jnp.exp(-g)).astype(BF16)
        k_state = (k * jnp.exp(g_last - g)).astype(BF16)
        decay = jnp.exp(g_last)
        scores = jnp.where(tril, _nt_dot(q_in, k_in), 0.0).astype(BF16)
        st = st_ref[...]
        o = jnp.dot(scores, v, preferred_element_type=F32) + _nt_dot(q_in, st.astype(BF16))
        v_t = v.astype(F32).T.astype(BF16)
        st_ref[...] = st * decay + jnp.dot(v_t, k_state, preferred_element_type=F32)
        o = o * lax.rsqrt(jnp.mean(o * o, axis=-1, keepdims=True) + RMS_EPS) * nw_ref[...]
        r = r_ref[pl.ds(r0, c), :].astype(F32)
        o_ref[pl.ds(r0, c), :] = (o * (r * jax.nn.sigmoid(r))).astype(BF16)
        return carry

    lax.fori_loop(0, n_chunks, chunk, 0)


def _gla_core(qkvr, logg, norm_w, b, s):
    t = b * s
    h, dk, dv = GLA_HEADS, GLA_DK, GLA_DV
    ts = min(GLA_ROWS, s)
    ns = s // ts
    k_off = (h * dk) // dk
    v_off = (2 * h * dk) // dv
    r_off = (2 * h * dk + h * dv) // dv
    return pl.pallas_call(
        _gla_kernel,
        grid=(b, h, ns),
        in_specs=[
            pl.BlockSpec((ts, dk), lambda bi, hi, si: (bi * ns + si, hi)),
            pl.BlockSpec((ts, dk), lambda bi, hi, si: (bi * ns + si, k_off + hi)),
            pl.BlockSpec((ts, dv), lambda bi, hi, si: (bi * ns + si, v_off + hi)),
            pl.BlockSpec((ts, dv), lambda bi, hi, si: (bi * ns + si, r_off + hi)),
            pl.BlockSpec((ts, dk), lambda bi, hi, si: (bi * ns + si, hi)),
            pl.BlockSpec((1, dv), lambda bi, hi, si: (0, 0)),
        ],
        out_specs=pl.BlockSpec((ts, dv), lambda bi, hi, si: (bi * ns + si, hi)),
        out_shape=jax.ShapeDtypeStruct((t, h * dv), BF16),
        scratch_shapes=[pltpu.VMEM((dv, dk), F32)],
        compiler_params=_params("arbitrary", "arbitrary", "arbitrary"),
        name="gla_core",
    )(qkvr, qkvr, qkvr, qkvr, logg, norm_w)


def _layer_norm(y, g, b):
    mu = jnp.mean(y, axis=-1, keepdims=True)
    yc = y - mu
    var = jnp.mean(yc * yc, axis=-1, keepdims=True)
    return yc * lax.rsqrt(var + LN_EPS) * g + b


def _proj_ln_kernel(o_ref, w_ref, x_ref, g_ref, b_ref, rw_ref, rb_ref, x1_ref, logit_ref):
    h = jnp.dot(o_ref[...], w_ref[...], preferred_element_type=F32)
    x1 = _layer_norm(DEEPNORM_ALPHA * x_ref[...] + h, g_ref[...], b_ref[...])
    x1_ref[...] = x1
    logit_ref[...] = jnp.dot(x1, rw_ref[...], preferred_element_type=F32, precision=HIGHEST) + rb_ref[...]


def _proj_ln(o, w_out, x2d, ln_g, ln_b, rw, rb):
    t, d = x2d.shape
    kdim = o.shape[1]
    tm = min(ROW_TILE, t)
    return pl.pallas_call(
        _proj_ln_kernel,
        grid=(t // tm,),
        in_specs=[
            pl.BlockSpec((tm, kdim), lambda i: (i, 0)),
            pl.BlockSpec((kdim, d), lambda i: (0, 0)),
            pl.BlockSpec((tm, d), lambda i: (i, 0)),
            pl.BlockSpec((1, d), lambda i: (0, 0)),
            pl.BlockSpec((1, d), lambda i: (0, 0)),
            pl.BlockSpec((d, LANES), lambda i: (0, 0)),
            pl.BlockSpec((1, LANES), lambda i: (0, 0)),
        ],
        out_specs=[
            pl.BlockSpec((tm, d), lambda i: (i, 0)),
            pl.BlockSpec((tm, LANES), lambda i: (i, 0)),
        ],
        out_shape=[jax.ShapeDtypeStruct((t, d), F32), jax.ShapeDtypeStruct((t, LANES), F32)],
        compiler_params=_params("arbitrary"),
        name="proj_ln",
    )(o, w_out, x2d, ln_g, ln_b, rw, rb)


def _route_kernel(lt_ref, lpos_ref, gate_ref, cnt_ref, before_ref, carry_ref):
    e, tb = lt_ref.shape

    @pl.when(pl.program_id(0) == 0)
    def _():
        carry_ref[...] = jnp.zeros_like(carry_ref)

    vals = lt_ref[...]
    eidx = lax.broadcasted_iota(jnp.int32, (e, tb), 0).astype(F32)
    top_v, hots = [], []
    for _ in range(TOP_K):
        m = jnp.max(vals, axis=0, keepdims=True)
        idx = jnp.min(jnp.where(vals == m, eidx, float(e)), axis=0, keepdims=True)
        hot = eidx == idx
        vals = jnp.where(hot, -jnp.inf, vals)
        top_v.append(m)
        hots.append(hot)
    ex = [jnp.exp(v - top_v[0]) for v in top_v]
    den = ex[0] + ex[1] + ex[2] + ex[3]
    gates = [x / den for x in ex]
    multi = jnp.zeros((e, tb), F32)
    for hot in hots:
        multi = multi + jnp.where(hot, 1.0, 0.0)
    ri = lax.broadcasted_iota(jnp.int32, (tb, tb), 0)
    ci = lax.broadcasted_iota(jnp.int32, (tb, tb), 1)
    upper = jnp.where(ri <= ci, 1.0, 0.0).astype(BF16)
    incl = jnp.dot(multi.astype(BF16), upper, preferred_element_type=F32)
    cnt = jnp.broadcast_to(incl[:, tb - 1:tb], (e, LANES))
    cnt = jnp.ceil(cnt * (1.0 / STRIP_ALIGN)) * STRIP_ALIGN
    er = lax.broadcasted_iota(jnp.int32, (e, e), 0)
    ec = lax.broadcasted_iota(jnp.int32, (e, e), 1)
    lower = jnp.where(er > ec, 1.0, 0.0).astype(BF16)
    cnt_hi = jnp.floor(cnt * (1.0 / BF16_EXACT_INT)) * BF16_EXACT_INT
    cnt_lo = cnt - cnt_hi
    start = (jnp.dot(lower, cnt_hi.astype(BF16), preferred_element_type=F32)
             + jnp.dot(lower, cnt_lo.astype(BF16), preferred_element_type=F32))[:, 0:1]
    where_to = start + incl - multi
    lpos = [jnp.sum(jnp.where(hot, where_to, 0.0), axis=0, keepdims=True) for hot in hots]
    lpos_ref[...] = jnp.concatenate(lpos + lpos, axis=0).astype(jnp.int32)
    gate_ref[...] = jnp.concatenate(gates + [jnp.zeros_like(g) for g in gates], axis=0)
    cnt_ref[...] = cnt
    before_ref[...] = carry_ref[...]
    carry_ref[...] = carry_ref[...] + cnt


def _route(logits_t):
    e, t = logits_t.shape
    tb = min(MOE_TOKENS, t)
    nblk = t // tb
    return pl.pallas_call(
        _route_kernel,
        grid=(nblk,),
        in_specs=[pl.BlockSpec((e, tb), lambda i: (0, i))],
        out_specs=[
            pl.BlockSpec((2 * TOP_K, tb), lambda i: (0, i)),
            pl.BlockSpec((2 * TOP_K, tb), lambda i: (0, i)),
            pl.BlockSpec((None, e, LANES), lambda i: (i, 0, 0)),
            pl.BlockSpec((None, e, LANES), lambda i: (i, 0, 0)),
        ],
        out_shape=[
            jax.ShapeDtypeStruct((2 * TOP_K, t), jnp.int32),
            jax.ShapeDtypeStruct((2 * TOP_K, t), F32),
            jax.ShapeDtypeStruct((nblk, e, LANES), F32),
            jax.ShapeDtypeStruct((nblk, e, LANES), F32),
        ],
        scratch_shapes=[pltpu.VMEM((e, LANES), F32)],
        compiler_params=_params("arbitrary"),
        name="route",
    )(logits_t)


def _local_rows(tb):
    return TOP_K * tb + STRIP_ALIGN * N_EXPERTS


def _for_each_strip_piece(cnt_ref, dst_ref, blk, max_rows, fn):
    lo_bit = STRIP_ALIGN.bit_length() - 1
    bits = max_rows.bit_length()

    def per_expert(e, local_row):
        c = cnt_ref[blk * N_EXPERTS + e]
        g = dst_ref[blk * N_EXPERTS + e]
        done = jnp.int32(0)
        for b in reversed(range(lo_bit, bits)):
            rows = 1 << b
            take = c & rows

            @pl.when(take != 0)
            def _(done=done, rows=rows):
                fn(pl.multiple_of(local_row + done, STRIP_ALIGN), pl.multiple_of(g + done, STRIP_ALIGN), rows)

            done = done + take
        return local_row + c

    lax.fori_loop(0, N_EXPERTS, per_expert, jnp.int32(0))


def _dispatch_kernel(cnt_ref, dst_ref, tail_ref, x_ref, lpos_ref, xb_hbm, ls_ref, zero_ref, sem):
    i = pl.program_id(0)
    tb = x_ref.shape[0]
    n = ls_ref.shape[0]
    bm = zero_ref.shape[0]

    @pl.when(i == 0)
    def _():
        zero_ref[...] = jnp.zeros_like(zero_ref)
        n_blocks = xb_hbm.shape[0] // bm
        n_used = tail_ref[N_EXPERTS]

        def zero_copy(row):
            return pltpu.make_async_copy(zero_ref, xb_hbm.at[pl.ds(pl.multiple_of(row, STRIP_ALIGN), bm)], sem)

        def start(e, c):
            @pl.when(tail_ref[e] >= 0)
            def _():
                zero_copy(tail_ref[e]).start()
            return c

        def wait(e, c):
            @pl.when(tail_ref[e] >= 0)
            def _():
                zero_copy(tail_ref[e]).wait()
            return c

        def start_unused(blk, c):
            zero_copy(blk * bm).start()
            return c

        def wait_unused(blk, c):
            zero_copy(blk * bm).wait()
            return c

        lax.fori_loop(0, N_EXPERTS, start, 0)
        lax.fori_loop(n_used, n_blocks, start_unused, 0)
        lax.fori_loop(0, N_EXPERTS, wait, 0)
        lax.fori_loop(n_used, n_blocks, wait_unused, 0)

    r = lax.broadcasted_iota(jnp.int32, (n, tb), 0)
    lp = lpos_ref[...]
    perm = jnp.where(r == lp[0:1], 1.0, 0.0)
    for k in range(1, TOP_K):
        perm = perm + jnp.where(r == lp[k:k + 1], 1.0, 0.0)
    ls_ref[...] = jnp.dot(perm.astype(BF16), x_ref[...].astype(BF16), preferred_element_type=F32)

    def piece(local_row, global_row, rows):
        return pltpu.make_async_copy(ls_ref.at[pl.ds(local_row, rows)], xb_hbm.at[pl.ds(global_row, rows)], sem)

    _for_each_strip_piece(cnt_ref, dst_ref, i, tb, lambda a, b, c: piece(a, b, c).start())
    _for_each_strip_piece(cnt_ref, dst_ref, i, tb, lambda a, b, c: piece(a, b, c).wait())


def _dispatch(x1, lpos, cnt_flat, dst_flat, tails, n_pad):
    t, d = x1.shape
    tb = min(MOE_TOKENS, t)
    return pl.pallas_call(
        _dispatch_kernel,
        grid_spec=pltpu.PrefetchScalarGridSpec(
            num_scalar_prefetch=3,
            grid=(t // tb,),
            in_specs=[
                pl.BlockSpec((tb, d), lambda i, c, g, tl: (i, 0)),
                pl.BlockSpec((2 * TOP_K, tb), lambda i, c, g, tl: (0, i)),
            ],
            out_specs=pl.BlockSpec(memory_space=pl.ANY),
            scratch_shapes=[
                pltpu.VMEM((_local_rows(tb), d), F32),
                pltpu.VMEM((EXPERT_ROWS, d), F32),
                pltpu.SemaphoreType.DMA,
            ],
        ),
        out_shape=jax.ShapeDtypeStruct((n_pad, d), F32),
        compiler_params=_params("arbitrary"),
        name="moe_dispatch",
    )(cnt_flat, dst_flat, tails, x1, lpos)


def _expert_kernel(be_ref, nu_ref, x_ref, wgu_ref, bgu_ref, wd_ref, bd_ref, y_ref, wgu_b, wd_b):
    i = pl.program_id(0)
    f = wd_ref.shape[0]

    @pl.when(i >= nu_ref[0])
    def _():
        y_ref[...] = jnp.zeros_like(y_ref)

    @pl.when(i < nu_ref[0])
    def _():
        prev = be_ref[jnp.maximum(i - 1, 0)]

        @pl.when((i == 0) | (be_ref[i] != prev))
        def _():
            wgu_b[...] = wgu_ref[...].astype(BF16)
            wd_b[...] = wd_ref[...].astype(BF16)

        xb = x_ref[...].astype(BF16)
        fc = 512
        acc = jnp.zeros(y_ref.shape, F32) + bd_ref[...]
        for j in range(0, f, fc):
            h_glu = jnp.dot(xb, wgu_b[:, j:j + fc], preferred_element_type=F32) + bgu_ref[:, j:j + fc]
            h_lin = (jnp.dot(xb, wgu_b[:, f + j:f + j + fc], preferred_element_type=F32)
                     + bgu_ref[:, f + j:f + j + fc])
            x_glu = jnp.minimum(h_glu, SWIGLU_LIMIT)
            x_lin = jnp.clip(h_lin, -SWIGLU_LIMIT, SWIGLU_LIMIT)
            act = x_glu * jax.nn.sigmoid(SWIGLU_ALPHA * x_glu) * (x_lin + 1.0)
            acc = acc + jnp.dot(act.astype(BF16), wd_b[j:j + fc, :], preferred_element_type=F32)
        y_ref[...] = acc


def _expert_ffn(xb, blk_e, n_used, w_gu, b_gu, w_down, b_down, layer):
    n_pad, d = xb.shape
    _, e, _, f2 = w_gu.shape
    f = f2 // 2
    bm = EXPERT_ROWS
    nb = n_pad // bm
    b_gu4 = b_gu.reshape(b_gu.shape[0], e, 1, f2)
    b_down4 = b_down.reshape(b_down.shape[0], e, 1, d)

    def row_map(i, be, nu):
        return (jnp.minimum(i, nu[0] - 1), 0)

    def w_map(i, be, nu):
        return (layer, be[i], 0, 0)

    return pl.pallas_call(
        _expert_kernel,
        grid_spec=pltpu.PrefetchScalarGridSpec(
            num_scalar_prefetch=2,
            grid=(nb,),
            in_specs=[
                pl.BlockSpec((bm, d), row_map),
                pl.BlockSpec((None, None, d, f2), w_map),
                pl.BlockSpec((None, None, 1, f2), w_map),
                pl.BlockSpec((None, None, f, d), w_map),
                pl.BlockSpec((None, None, 1, d), w_map),
            ],
            out_specs=pl.BlockSpec((bm, d), lambda i, be, nu: (i, 0)),
            scratch_shapes=[pltpu.VMEM((d, f2), BF16), pltpu.VMEM((f, d), BF16)],
        ),
        out_shape=jax.ShapeDtypeStruct((n_pad, d), F32),
        compiler_params=_params("arbitrary"),
        name="moe_experts",
    )(blk_e, n_used, xb, w_gu, b_gu4, w_down, b_down4)


def _combine_kernel(cnt_ref, dst_ref, x_ref, lpos_ref, gate_ref, g_ref, b_ref, yb_hbm, o_ref, ly_ref, sem):
    i = pl.program_id(0)
    tb = x_ref.shape[0]
    n = ly_ref.shape[0]

    @pl.when(i == 0)
    def _():
        ly_ref[...] = jnp.zeros_like(ly_ref)

    def piece(local_row, global_row, rows):
        return pltpu.make_async_copy(yb_hbm.at[pl.ds(global_row, rows)], ly_ref.at[pl.ds(local_row, rows)], sem)

    _for_each_strip_piece(cnt_ref, dst_ref, i, tb, lambda a, b, c: piece(a, b, c).start())
    _for_each_strip_piece(cnt_ref, dst_ref, i, tb, lambda a, b, c: piece(a, b, c).wait())

    c = lax.broadcasted_iota(jnp.int32, (tb, n), 1)
    w = jnp.where(c == lpos_ref[:, 0:1], gate_ref[:, 0:1], 0.0)
    for k in range(1, TOP_K):
        w = w + jnp.where(c == lpos_ref[:, k:k + 1], gate_ref[:, k:k + 1], 0.0)
    m = jnp.dot(w.astype(BF16), ly_ref[...].astype(BF16), preferred_element_type=F32)
    o_ref[...] = _layer_norm(DEEPNORM_ALPHA * x_ref[...] + m, g_ref[...], b_ref[...])


def _combine(x1, lpos_t, gates_t, ln_g, ln_b, cnt_flat, dst_flat, yb):
    t, d = x1.shape
    tb = min(MOE_TOKENS, t)
    return pl.pallas_call(
        _combine_kernel,
        grid_spec=pltpu.PrefetchScalarGridSpec(
            num_scalar_prefetch=2,
            grid=(t // tb,),
            in_specs=[
                pl.BlockSpec((tb, d), lambda i, c, g: (i, 0)),
                pl.BlockSpec((tb, 2 * TOP_K), lambda i, c, g: (i, 0)),
                pl.BlockSpec((tb, 2 * TOP_K), lambda i, c, g: (i, 0)),
                pl.BlockSpec((1, d), lambda i, c, g: (0, 0)),
                pl.BlockSpec((1, d), lambda i, c, g: (0, 0)),
                pl.BlockSpec(memory_space=pl.ANY),
            ],
            out_specs=pl.BlockSpec((tb, d), lambda i, c, g: (i, 0)),
            scratch_shapes=[
                pltpu.VMEM((_local_rows(tb), d), F32),
                pltpu.SemaphoreType.DMA,
            ],
        ),
        out_shape=jax.ShapeDtypeStruct((t, d), F32),
        compiler_params=_params("arbitrary"),
        name="moe_combine",
    )(cnt_flat, dst_flat, x1, lpos_t, gates_t, ln_g, ln_b, yb)


def _moe(x1, logits, ln_g, ln_b, w_gu, b_gu, w_down, b_down, layer):
    t, d = x1.shape
    e = N_EXPERTS
    bm = EXPERT_ROWS
    lpos, gates, cnt_blk, before_blk = _route(logits[:, :e].T)
    cnt_blk = cnt_blk[:, :, 0].astype(jnp.int32)
    before_blk = before_blk[:, :, 0].astype(jnp.int32)
    cnt = before_blk[-1] + cnt_blk[-1]
    padded = (cnt + bm - 1) // bm * bm
    pad_end = jnp.cumsum(padded)
    pad_start = pad_end - padded
    dst_blk = pad_start[None, :] + before_blk
    nb = -(-(t * TOP_K + (STRIP_ALIGN - 1) * cnt_blk.shape[0] * e) // bm) + e
    n_used = (pad_end[-1] // bm).astype(jnp.int32)
    blk_start = jnp.arange(nb, dtype=jnp.int32) * bm
    blk_e = jnp.minimum(jnp.sum(pad_end[None, :] <= blk_start[:, None], axis=1), e - 1).astype(jnp.int32)
    blk_e = jnp.where(jnp.arange(nb) < n_used, blk_e, blk_e[jnp.maximum(n_used - 1, 0)])
    tails = jnp.concatenate([jnp.where(padded > 0, pad_end - bm, -1), n_used.reshape(1)]).astype(jnp.int32)
    cnt_flat = cnt_blk.reshape(-1)
    dst_flat = dst_blk.reshape(-1).astype(jnp.int32)
    xb = _dispatch(x1, lpos, cnt_flat, dst_flat, tails, nb * bm)
    yb = _expert_ffn(xb, blk_e, n_used.reshape(1), w_gu, b_gu, w_down, b_down, layer)
    return _combine(x1, lpos.T, gates.T, ln_g, ln_b, cnt_flat, dst_flat, yb)


def _qkv_rope_kernel(x_ref, w_ref, pos_ref, inv_ref, out_ref):
    xb = x_ref[...].astype(BF16)
    tm = x_ref.shape[0]
    n_rope = 2 * (2 * DIFF_HEADS * DIFF_HD)
    ang = pos_ref[...].astype(F32) * inv_ref[...]
    cos = jnp.cos(ang)
    sin = jnp.sin(ang)
    lane = lax.broadcasted_iota(jnp.int32, (tm, LANES), 1)
    first_half = (lane % DIFF_HD) < (DIFF_HD // 2)
    sin_signed = jnp.where(first_half, -sin, sin)
    q_scale = DIFF_HD ** -0.5
    n = w_ref.shape[1]
    for j in range(0, n, LANES):
        c = jnp.dot(xb, w_ref[:, j:j + LANES], preferred_element_type=F32)
        if j < n_rope:
            rot = jnp.where(first_half, pltpu.roll(c, LANES - DIFF_HD // 2, 1), pltpu.roll(c, DIFF_HD // 2, 1))
            c = c * cos + rot * sin_signed
            if j < n_rope // 2:
                c = c * q_scale
        out_ref[:, j:j + LANES] = c.astype(BF16)


def _qkv_rope(x2d, w_qkv, pos_col, inv_row):
    t, d = x2d.shape
    n = w_qkv.shape[1]
    tm = min(ROW_TILE, t)
    return pl.pallas_call(
        _qkv_rope_kernel,
        grid=(t // tm,),
        in_specs=[
            pl.BlockSpec((tm, d), lambda i: (i, 0)),
            pl.BlockSpec((d, n), lambda i: (0, 0)),
            pl.BlockSpec((tm, 1), lambda i: (i, 0)),
            pl.BlockSpec((1, LANES), lambda i: (0, 0)),
        ],
        out_specs=pl.BlockSpec((tm, n), lambda i: (i, 0)),
        out_shape=jax.ShapeDtypeStruct((t, n), BF16),
        compiler_params=_params("arbitrary"),
        name="qkv_rope",
    )(x2d, w_qkv, pos_col, inv_row)


def _diff_attn_kernel(q_ref, k_ref, v_ref, lam_ref, sw_ref, o_ref, qs_ref, m_ref, l_ref, acc_ref, *, lambda_init):
    qi = pl.program_id(2)
    tq = q_ref.shape[0]
    tk = ATTN_K if k_ref.shape[0] >= ATTN_K else k_ref.shape[0]
    q = q_ref[...]
    lane = lax.broadcasted_iota(jnp.int32, q.shape, 1)
    zero = jnp.zeros_like(q)
    qs_ref[0:tq, :] = jnp.where(lane < DIFF_HD, q, zero)
    qs_ref[tq:2 * tq, :] = jnp.where(lane >= DIFF_HD, q, zero)
    m_ref[...] = jnp.full(m_ref.shape, -jnp.inf, F32)
    l_ref[...] = jnp.zeros_like(l_ref)
    acc_ref[...] = jnp.zeros_like(acc_ref)

    def block(kj, masked):
        k0 = pl.multiple_of(kj * tk, tk)
        s = _nt_dot(qs_ref[...], k_ref[pl.ds(k0, tk), :])
        if masked:
            qpos = qi * tq + lax.broadcasted_iota(jnp.int32, (2 * tq, tk), 0) % tq
            kpos = k0 + lax.broadcasted_iota(jnp.int32, (2 * tq, tk), 1)
            s = jnp.where(kpos <= qpos, s, -jnp.inf)
        m_old = m_ref[...]
        m_new = jnp.maximum(m_old, jnp.max(s, axis=-1, keepdims=True))
        p = jnp.exp(s - jnp.concatenate([m_new] * (tk // LANES), axis=1))
        alpha = jnp.exp(m_old - m_new)
        l_ref[...] = alpha * l_ref[...] + jnp.sum(p, axis=-1, keepdims=True)
        acc_ref[...] = alpha * acc_ref[...] + jnp.dot(p.astype(BF16), v_ref[pl.ds(k0, tk), :],
                                                      preferred_element_type=F32)
        m_ref[...] = m_new

    n_full = (qi * tq) // tk

    def body(kj, c):
        block(kj, False)
        return c

    lax.fori_loop(0, n_full, body, 0)
    for d in range(tq // tk):
        block(n_full + d, True)

    lam = lam_ref[...]
    lam_full = (jnp.exp(jnp.sum(lam[0:1] * lam[1:2], axis=-1, keepdims=True))
                - jnp.exp(jnp.sum(lam[2:3] * lam[3:4], axis=-1, keepdims=True)) + lambda_init)
    o1 = acc_ref[0:tq, :] / l_ref[0:tq, :]
    o2 = acc_ref[tq:2 * tq, :] / l_ref[tq:2 * tq, :]
    o = o1 - lam_full * o2
    o = o * lax.rsqrt(jnp.mean(o * o, axis=-1, keepdims=True) + RMS_EPS) * sw_ref[...]
    o_ref[...] = (o * (1.0 - lambda_init)).astype(BF16)


def _diff_attn(qkv, lam, subln_w, b, s, lambda_init):
    t = b * s
    h, w = DIFF_HEADS, 2 * DIFF_HD
    tq = min(ATTN_Q, s)
    nq = s // tq
    return pl.pallas_call(
        functools.partial(_diff_attn_kernel, lambda_init=lambda_init),
        grid=(b, h, nq),
        in_specs=[
            pl.BlockSpec((tq, w), lambda bi, hi, qi: (bi * nq + qi, hi)),
            pl.BlockSpec((s, w), lambda bi, hi, qi: (bi, h + hi)),
            pl.BlockSpec((s, DIFF_VD), lambda bi, hi, qi: (bi, 2 * h + hi)),
            pl.BlockSpec((4, DIFF_HD), lambda bi, hi, qi: (0, 0)),
            pl.BlockSpec((1, DIFF_VD), lambda bi, hi, qi: (0, 0)),
        ],
        out_specs=pl.BlockSpec((tq, DIFF_VD), lambda bi, hi, qi: (bi * nq + qi, hi)),
        out_shape=jax.ShapeDtypeStruct((t, h * DIFF_VD), BF16),
        scratch_shapes=[
            pltpu.VMEM((2 * tq, w), BF16),
            pltpu.VMEM((2 * tq, LANES), F32),
            pltpu.VMEM((2 * tq, LANES), F32),
            pltpu.VMEM((2 * tq, DIFF_VD), F32),
        ],
        compiler_params=_params("arbitrary", "arbitrary", "arbitrary"),
        name="diff_attn",
    )(qkv, qkv, qkv, lam, subln_w)


def _pad_cols(a, n):
    return jnp.pad(a, ((0, 0), (0, n - a.shape[1])))


def kernel(x, positions, a_w_in, a_w_gk2, a_b_gk, a_norm_w, a_w_out, kv_w, b_w_q, b_lambda, b_subln_w,
           b_w_out, ln1_g, ln1_b, ln2_g, ln2_b, router_w, router_b, moe_w_gu, moe_b_gu, moe_w_down,
           moe_b_down):
    b, s, d = x.shape
    t = b * s
    xc = x.reshape(t, d)
    n_main = 2 * GLA_HEADS * GLA_DK + 2 * GLA_HEADS * GLA_DV
    half = DIFF_HD // 2
    inv = ROPE_THETA ** (-jnp.arange(half, dtype=F32) * 2.0 / DIFF_HD)
    inv_row = jnp.tile(inv, LANES // half).reshape(1, LANES)
    pos_col = positions.reshape(t, 1)
    assert DEPTH - N_A_LAYERS == 1
    for l in range(DEPTH):
        if l < N_A_LAYERS:
            w_in = a_w_in[l]
            qkvr, logg = _gla_proj(
                xc, w_in[:, :n_main].astype(BF16), _pad_cols(w_in[:, n_main:], LANES).astype(BF16),
                jnp.pad(a_w_gk2[l], ((0, LANES - GLA_GATE_RANK), (0, 0))), a_b_gk[l].reshape(1, -1))
            o = _gla_core(qkvr, logg, a_norm_w[l].reshape(1, -1), b, s)
            w_out = a_w_out[l]
        else:
            j = l - N_A_LAYERS
            lambda_init = 0.8 - 0.6 * math.exp(-0.3 * l)
            w_qkv = jnp.concatenate([b_w_q[j], kv_w], axis=1).astype(BF16)
            qkv = _qkv_rope(xc, w_qkv, pos_col, inv_row)
            o = _diff_attn(qkv, b_lambda[j], b_subln_w[j].reshape(1, -1), b, s, lambda_init)
            w_out = b_w_out[j]
        x1, logits = _proj_ln(o, w_out.astype(BF16), xc, ln1_g[l].reshape(1, -1), ln1_b[l].reshape(1, -1),
                              _pad_cols(router_w[l], LANES), _pad_cols(router_b[l].reshape(1, -1), LANES))
        xc = _moe(x1, logits, ln2_g[l].reshape(1, -1), ln2_b[l].reshape(1, -1),
                  moe_w_gu, moe_b_gu, moe_w_down, moe_b_down, l)
    return xc.reshape(b, s, d)
```

```python
import functools
import math

import jax
import jax.numpy as jnp
from jax import lax
from jax.experimental import pallas as pl
from jax.experimental.pallas import tpu as pltpu

F32 = jnp.float32
BF16 = jnp.bfloat16
HIGHEST = lax.Precision.HIGHEST

DEPTH = 2
N_A_LAYERS = DEPTH // 2
GLA_HEADS = 4
GLA_DK = 128
GLA_DV = 256
GLA_GATE_RANK = 16
GLA_GATE_TAU = 16.0
GLA_CHUNK = 64
DIFF_HEADS = 8
DIFF_HD = 64
DIFF_VD = 128
ROPE_THETA = 10000.0
N_EXPERTS = 32
TOP_K = 4
SWIGLU_LIMIT = 7.0
SWIGLU_ALPHA = 1.702
LN_EPS = 1e-5
RMS_EPS = 1e-5
DEEPNORM_ALPHA = (2.0 * DEPTH) ** 0.25

LANES = 128
SUBLANES = 8
BF16_EXACT_INT = 256
STRIP_ALIGN = SUBLANES
VMEM_LIMIT_BYTES = 56 * 1024 * 1024

ROW_TILE = 512
EXPERT_ROWS = 512
MOE_TOKENS = 512
ATTN_Q = 512
ATTN_K = 512
GLA_ROWS = 512


def _params(*sem):
    return pltpu.CompilerParams(dimension_semantics=sem, vmem_limit_bytes=VMEM_LIMIT_BYTES)


def _nt_dot(a, b):
    return lax.dot_general(a, b, (((1,), (1,)), ((), ())), preferred_element_type=F32)


def _gla_proj_kernel(x_ref, w_ref, wg_ref, wgk2_ref, bgk_ref, qkvr_ref, logg_ref):
    xb = x_ref[...].astype(BF16)
    n = w_ref.shape[1]
    for j in range(0, n, 512):
        qkvr_ref[:, j:j + 512] = jnp.dot(xb, w_ref[:, j:j + 512],
                                         preferred_element_type=F32).astype(BF16)
    gk_low = jnp.dot(xb, wg_ref[...], preferred_element_type=F32)
    z = jnp.dot(gk_low, wgk2_ref[...], preferred_element_type=F32, precision=HIGHEST) + bgk_ref[...]
    logg_ref[...] = (jnp.minimum(z, 0.0) - jnp.log1p(jnp.exp(-jnp.abs(z)))) / GLA_GATE_TAU


def _gla_proj(x2d, w_main, w_gate, w_gk2, b_gk):
    t, d = x2d.shape
    n = w_main.shape[1]
    hk = w_gk2.shape[1]
    tm = min(ROW_TILE, t)
    return pl.pallas_call(
        _gla_proj_kernel,
        grid=(t // tm,),
        in_specs=[
            pl.BlockSpec((tm, d), lambda i: (i, 0)),
            pl.BlockSpec((d, n), lambda i: (0, 0)),
            pl.BlockSpec((d, LANES), lambda i: (0, 0)),
            pl.BlockSpec((LANES, hk), lambda i: (0, 0)),
            pl.BlockSpec((1, hk), lambda i: (0, 0)),
        ],
        out_specs=[
            pl.BlockSpec((tm, n), lambda i: (i, 0)),
            pl.BlockSpec((tm, hk), lambda i: (i, 0)),
        ],
        out_shape=[jax.ShapeDtypeStruct((t, n), BF16), jax.ShapeDtypeStruct((t, hk), F32)],
        compiler_params=_params("arbitrary"),
        name="gla_proj",
    )(x2d, w_main, w_gate, w_gk2, b_gk)


def _gla_kernel(q_ref, k_ref, v_ref, r_ref, lg_ref, nw_ref, o_ref, st_ref):
    c = GLA_CHUNK

    @pl.when(pl.program_id(2) == 0)
    def _():
        st_ref[...] = jnp.zeros_like(st_ref)

    row = lax.broadcasted_iota(jnp.int32, (c, c), 0)
    col = lax.broadcasted_iota(jnp.int32, (c, c), 1)
    tril = row >= col
    row_id = lax.broadcasted_iota(jnp.int32, (c, GLA_DK), 0)
    n_chunks = q_ref.shape[0] // c

    st = st_ref[...]
    for ci in range(n_chunks):
        r0 = ci * c
        lg = lg_ref[pl.ds(r0, c), :]
        g = lg
        shift = 1
        while shift < c:
            g = g + jnp.where(row_id >= shift, pltpu.roll(g, shift, 0), 0.0)
            shift *= 2
        g_last = g[c - 1:c, :]
        q = q_ref[pl.ds(r0, c), :].astype(F32) * (GLA_DK ** -0.5)
        k = k_ref[pl.ds(r0, c), :].astype(F32)
        v = v_ref[pl.ds(r0, c), :]
        q_in = (q * jnp.exp(g)).astype(BF16)
        k_in = (k * jnp.exp(-g)).astype(BF16)
        k_state = (k * jnp.exp(g_last - g)).astype(BF16)
        decay = jnp.exp(g_last)
        scores = jnp.where(tril, _nt_dot(q_in, k_in), 0.0).astype(BF16)
        o = jnp.dot(scores, v, preferred_element_type=F32) + _nt_dot(q_in, st.astype(BF16))
        v_t = v.astype(F32).T.astype(BF16)
        st = st * decay + jnp.dot(v_t, k_state, preferred_element_type=F32)
        o = o * lax.rsqrt(jnp.mean(o * o, axis=-1, keepdims=True) + RMS_EPS) * nw_ref[...]
        r = r_ref[pl.ds(r0, c), :].astype(F32)
        o_ref[pl.ds(r0, c), :] = (o * (r * jax.nn.sigmoid(r))).astype(BF16)
    st_ref[...] = st


def _gla_core(qkvr, logg, norm_w, b, s):
    t = b * s
    h, dk, dv = GLA_HEADS, GLA_DK, GLA_DV
    ts = min(GLA_ROWS, s)
    ns = s // ts
    k_off = (h * dk) // dk
    v_off = (2 * h * dk) // dv
    r_off = (2 * h * dk + h * dv) // dv
    return pl.pallas_call(
        _gla_kernel,
        grid=(b, h, ns),
        in_specs=[
            pl.BlockSpec((ts, dk), lambda bi, hi, si: (bi * ns + si, hi)),
            pl.BlockSpec((ts, dk), lambda bi, hi, si: (bi * ns + si, k_off + hi)),
            pl.BlockSpec((ts, dv), lambda bi, hi, si: (bi * ns + si, v_off + hi)),
            pl.BlockSpec((ts, dv), lambda bi, hi, si: (bi * ns + si, r_off + hi)),
            pl.BlockSpec((ts, dk), lambda bi, hi, si: (bi * ns + si, hi)),
            pl.BlockSpec((1, dv), lambda bi, hi, si: (0, 0)),
        ],
        out_specs=pl.BlockSpec((ts, dv), lambda bi, hi, si: (bi * ns + si, hi)),
        out_shape=jax.ShapeDtypeStruct((t, h * dv), BF16),
        scratch_shapes=[pltpu.VMEM((dv, dk), F32)],
        compiler_params=_params("arbitrary", "arbitrary", "arbitrary"),
        name="gla_core",
    )(qkvr, qkvr, qkvr, qkvr, logg, norm_w)


def _layer_norm(y, g, b):
    mu = jnp.mean(y, axis=-1, keepdims=True)
    yc = y - mu
    var = jnp.mean(yc * yc, axis=-1, keepdims=True)
    return yc * lax.rsqrt(var + LN_EPS) * g + b


def _proj_ln_kernel(o_ref, w_ref, x_ref, g_ref, b_ref, rw_ref, rb_ref, x1_ref, logit_ref):
    h = jnp.dot(o_ref[...], w_ref[...], preferred_element_type=F32)
    x1 = _layer_norm(DEEPNORM_ALPHA * x_ref[...] + h, g_ref[...], b_ref[...])
    x1_ref[...] = x1
    xh = x1.astype(BF16)
    xl = (x1 - xh.astype(F32)).astype(BF16)
    hw = jnp.dot(xh, rw_ref[...], preferred_element_type=F32)
    lw = jnp.dot(xl, rw_ref[:, 0:LANES], preferred_element_type=F32)
    logit_ref[...] = hw[:, 0:LANES] + hw[:, LANES:2 * LANES] + lw + rb_ref[...]


def _proj_ln(o, w_out, x2d, ln_g, ln_b, rw, rb):
    t, d = x2d.shape
    kdim = o.shape[1]
    tm = min(ROW_TILE, t)
    return pl.pallas_call(
        _proj_ln_kernel,
        grid=(t // tm,),
        in_specs=[
            pl.BlockSpec((tm, kdim), lambda i: (i, 0)),
            pl.BlockSpec((kdim, d), lambda i: (0, 0)),
            pl.BlockSpec((tm, d), lambda i: (i, 0)),
            pl.BlockSpec((1, d), lambda i: (0, 0)),
            pl.BlockSpec((1, d), lambda i: (0, 0)),
            pl.BlockSpec((d, 2 * LANES), lambda i: (0, 0)),
            pl.BlockSpec((1, LANES), lambda i: (0, 0)),
        ],
        out_specs=[
            pl.BlockSpec((tm, d), lambda i: (i, 0)),
            pl.BlockSpec((tm, LANES), lambda i: (i, 0)),
        ],
        out_shape=[jax.ShapeDtypeStruct((t, d), F32), jax.ShapeDtypeStruct((t, LANES), F32)],
        compiler_params=_params("arbitrary"),
        name="proj_ln",
    )(o, w_out, x2d, ln_g, ln_b, rw, rb)


def _route_kernel(lt_ref, lpos_ref, gate_ref, cnt_ref, before_ref, carry_ref):
    e, tb = lt_ref.shape

    @pl.when(pl.program_id(0) == 0)
    def _():
        carry_ref[...] = jnp.zeros_like(carry_ref)

    vals = lt_ref[...]
    eidx = lax.broadcasted_iota(jnp.int32, (e, tb), 0).astype(F32)
    top_v, hots = [], []
    for _ in range(TOP_K):
        m = jnp.max(vals, axis=0, keepdims=True)
        idx = jnp.min(jnp.where(vals == m, eidx, float(e)), axis=0, keepdims=True)
        hot = eidx == idx
        vals = jnp.where(hot, -jnp.inf, vals)
        top_v.append(m)
        hots.append(hot)
    ex = [jnp.exp(v - top_v[0]) for v in top_v]
    den = ex[0] + ex[1] + ex[2] + ex[3]
    gates = [x / den for x in ex]
    multi = jnp.zeros((e, tb), F32)
    for hot in hots:
        multi = multi + jnp.where(hot, 1.0, 0.0)
    ri = lax.broadcasted_iota(jnp.int32, (tb, tb), 0)
    ci = lax.broadcasted_iota(jnp.int32, (tb, tb), 1)
    upper = jnp.where(ri <= ci, 1.0, 0.0).astype(BF16)
    incl = jnp.dot(multi.astype(BF16), upper, preferred_element_type=F32)
    cnt = jnp.broadcast_to(incl[:, tb - 1:tb], (e, LANES))
    cnt = jnp.ceil(cnt * (1.0 / STRIP_ALIGN)) * STRIP_ALIGN
    er = lax.broadcasted_iota(jnp.int32, (e, e), 0)
    ec = lax.broadcasted_iota(jnp.int32, (e, e), 1)
    lower = jnp.where(er > ec, 1.0, 0.0).astype(BF16)
    cnt_hi = jnp.floor(cnt * (1.0 / BF16_EXACT_INT)) * BF16_EXACT_INT
    cnt_lo = cnt - cnt_hi
    start = (jnp.dot(lower, cnt_hi.astype(BF16), preferred_element_type=F32)
             + jnp.dot(lower, cnt_lo.astype(BF16), preferred_element_type=F32))[:, 0:1]
    where_to = start + incl - multi
    lpos = [jnp.sum(jnp.where(hot, where_to, 0.0), axis=0, keepdims=True) for hot in hots]
    lpos_ref[...] = jnp.concatenate(lpos + lpos, axis=0).astype(jnp.int32)
    gate_ref[...] = jnp.concatenate(gates + [jnp.zeros_like(g) for g in gates], axis=0)
    cnt_ref[...] = cnt
    before_ref[...] = carry_ref[...]
    carry_ref[...] = carry_ref[...] + cnt


def _route(logits_t):
    e, t = logits_t.shape
    tb = min(MOE_TOKENS, t)
    nblk = t // tb
    return pl.pallas_call(
        _route_kernel,
        grid=(nblk,),
        in_specs=[pl.BlockSpec((e, tb), lambda i: (0, i))],
        out_specs=[
            pl.BlockSpec((2 * TOP_K, tb), lambda i: (0, i)),
            pl.BlockSpec((2 * TOP_K, tb), lambda i: (0, i)),
            pl.BlockSpec((None, e, LANES), lambda i: (i, 0, 0)),
            pl.BlockSpec((None, e, LANES), lambda i: (i, 0, 0)),
        ],
        out_shape=[
            jax.ShapeDtypeStruct((2 * TOP_K, t), jnp.int32),
            jax.ShapeDtypeStruct((2 * TOP_K, t), F32),
            jax.ShapeDtypeStruct((nblk, e, LANES), F32),
            jax.ShapeDtypeStruct((nblk, e, LANES), F32),
        ],
        scratch_shapes=[pltpu.VMEM((e, LANES), F32)],
        compiler_params=_params("arbitrary"),
        name="route",
    )(logits_t)


def _local_rows(tb):
    return TOP_K * tb + STRIP_ALIGN * N_EXPERTS


def _for_each_strip_piece(cnt_ref, dst_ref, blk, max_rows, fn):
    lo_bit = STRIP_ALIGN.bit_length() - 1
    bits = max_rows.bit_length()

    def per_expert(e, local_row):
        c = cnt_ref[blk * N_EXPERTS + e]
        g = dst_ref[blk * N_EXPERTS + e]
        done = jnp.int32(0)
        for b in reversed(range(lo_bit, bits)):
            rows = 1 << b
            take = c & rows

            @pl.when(take != 0)
            def _(done=done, rows=rows):
                fn(pl.multiple_of(local_row + done, STRIP_ALIGN), pl.multiple_of(g + done, STRIP_ALIGN), rows)

            done = done + take
        return local_row + c

    lax.fori_loop(0, N_EXPERTS, per_expert, jnp.int32(0))


def _dispatch_kernel(cnt_ref, dst_ref, tail_ref, x_ref, lpos_ref, xb_hbm, ls_ref, zero_ref, sem, zero_sem):
    i = pl.program_id(0)
    tb = x_ref.shape[0]
    n = ls_ref.shape[1]
    bm = zero_ref.shape[0]

    @pl.when(i == 0)
    def _():
        zero_ref[...] = jnp.zeros_like(zero_ref)
        n_blocks = xb_hbm.shape[0] // bm
        n_used = tail_ref[N_EXPERTS]

        def zero_copy(row):
            return pltpu.make_async_copy(zero_ref, xb_hbm.at[pl.ds(pl.multiple_of(row, STRIP_ALIGN), bm)], zero_sem)

        def start(e, c):
            @pl.when(tail_ref[e] >= 0)
            def _():
                zero_copy(tail_ref[e]).start()
            return c

        def wait(e, c):
            @pl.when(tail_ref[e] >= 0)
            def _():
                zero_copy(tail_ref[e]).wait()
            return c

        def start_unused(blk, c):
            zero_copy(blk * bm).start()
            return c

        def wait_unused(blk, c):
            zero_copy(blk * bm).wait()
            return c

        lax.fori_loop(0, N_EXPERTS, start, 0)
        lax.fori_loop(n_used, n_blocks, start_unused, 0)
        lax.fori_loop(0, N_EXPERTS, wait, 0)
        lax.fori_loop(n_used, n_blocks, wait_unused, 0)

    r = lax.broadcasted_iota(jnp.int32, (n, tb), 0)
    lp = lpos_ref[...]
    perm = jnp.where(r == lp[0:1], 1.0, 0.0)
    for k in range(1, TOP_K):
        perm = perm + jnp.where(r == lp[k:k + 1], 1.0, 0.0)
    slot = i % 2
    ls_ref[slot] = jnp.dot(perm.astype(BF16), x_ref[...].astype(BF16), preferred_element_type=F32)

    def strips(blk, blk_slot, act):
        def piece(local_row, global_row, rows):
            cp = pltpu.make_async_copy(ls_ref.at[blk_slot, pl.ds(local_row, rows)],
                                       xb_hbm.at[pl.ds(global_row, rows)], sem.at[blk_slot])
            cp.start() if act == "start" else cp.wait()
        _for_each_strip_piece(cnt_ref, dst_ref, blk, tb, piece)

    strips(i, slot, "start")

    @pl.when(i > 0)
    def _():
        strips(i - 1, 1 - slot, "wait")

    @pl.when(i == pl.num_programs(0) - 1)
    def _():
        strips(i, slot, "wait")


def _dispatch(x1, lpos, cnt_flat, dst_flat, tails, n_pad):
    t, d = x1.shape
    tb = min(MOE_TOKENS, t)
    return pl.pallas_call(
        _dispatch_kernel,
        grid_spec=pltpu.PrefetchScalarGridSpec(
            num_scalar_prefetch=3,
            grid=(t // tb,),
            in_specs=[
                pl.BlockSpec((tb, d), lambda i, c, g, tl: (i, 0)),
                pl.BlockSpec((2 * TOP_K, tb), lambda i, c, g, tl: (0, i)),
            ],
            out_specs=pl.BlockSpec(memory_space=pl.ANY),
            scratch_shapes=[
                pltpu.VMEM((2, _local_rows(tb), d), F32),
                pltpu.VMEM((EXPERT_ROWS, d), F32),
                pltpu.SemaphoreType.DMA((2,)),
                pltpu.SemaphoreType.DMA,
            ],
        ),
        out_shape=jax.ShapeDtypeStruct((n_pad, d), F32),
        compiler_params=_params("arbitrary"),
        name="moe_dispatch",
    )(cnt_flat, dst_flat, tails, x1, lpos)


def _expert_kernel(be_ref, nu_ref, x_ref, wgu_ref, bgu_ref, wd_ref, bd_ref, y_ref, wgu_b, wd_b):
    i = pl.program_id(0)
    f = wd_ref.shape[0]

    @pl.when(i >= nu_ref[0])
    def _():
        y_ref[...] = jnp.zeros_like(y_ref)

    @pl.when(i < nu_ref[0])
    def _():
        prev = be_ref[jnp.maximum(i - 1, 0)]

        @pl.when((i == 0) | (be_ref[i] != prev))
        def _():
            wgu_b[...] = wgu_ref[...].astype(BF16)
            wd_b[...] = wd_ref[...].astype(BF16)

        xb = x_ref[...].astype(BF16)
        fc = 512
        acc = jnp.zeros(y_ref.shape, F32) + bd_ref[...]
        for j in range(0, f, fc):
            h_glu = jnp.dot(xb, wgu_b[:, j:j + fc], preferred_element_type=F32) + bgu_ref[:, j:j + fc]
            h_lin = (jnp.dot(xb, wgu_b[:, f + j:f + j + fc], preferred_element_type=F32)
                     + bgu_ref[:, f + j:f + j + fc])
            x_glu = jnp.minimum(h_glu, SWIGLU_LIMIT)
            x_lin = jnp.clip(h_lin, -SWIGLU_LIMIT, SWIGLU_LIMIT)
            act = x_glu * jax.nn.sigmoid(SWIGLU_ALPHA * x_glu) * (x_lin + 1.0)
            acc = acc + jnp.dot(act.astype(BF16), wd_b[j:j + fc, :], preferred_element_type=F32)
        y_ref[...] = acc


def _expert_ffn(xb, blk_e, n_used, w_gu, b_gu, w_down, b_down, layer):
    n_pad, d = xb.shape
    _, e, _, f2 = w_gu.shape
    f = f2 // 2
    bm = EXPERT_ROWS
    nb = n_pad // bm
    b_gu4 = b_gu.reshape(b_gu.shape[0], e, 1, f2)
    b_down4 = b_down.reshape(b_down.shape[0], e, 1, d)

    def row_map(i, be, nu):
        return (jnp.minimum(i, nu[0] - 1), 0)

    def w_map(i, be, nu):
        return (layer, be[i], 0, 0)

    return pl.pallas_call(
        _expert_kernel,
        grid_spec=pltpu.PrefetchScalarGridSpec(
            num_scalar_prefetch=2,
            grid=(nb,),
            in_specs=[
                pl.BlockSpec((bm, d), row_map),
                pl.BlockSpec((None, None, d, f2), w_map),
                pl.BlockSpec((None, None, 1, f2), w_map),
                pl.BlockSpec((None, None, f, d), w_map),
                pl.BlockSpec((None, None, 1, d), w_map),
            ],
            out_specs=pl.BlockSpec((bm, d), lambda i, be, nu: (i, 0)),
            scratch_shapes=[pltpu.VMEM((d, f2), BF16), pltpu.VMEM((f, d), BF16)],
        ),
        out_shape=jax.ShapeDtypeStruct((n_pad, d), F32),
        compiler_params=_params("arbitrary"),
        name="moe_experts",
    )(blk_e, n_used, xb, w_gu, b_gu4, w_down, b_down4)


def _combine_kernel(cnt_ref, dst_ref, x_ref, lpos_ref, gate_ref, g_ref, b_ref, yb_hbm, o_ref, ly_ref, sem):
    i = pl.program_id(0)
    tb = x_ref.shape[0]
    n = ly_ref.shape[1]
    slot = i % 2

    def strips(blk, blk_slot, act):
        def piece(local_row, global_row, rows):
            cp = pltpu.make_async_copy(yb_hbm.at[pl.ds(global_row, rows)],
                                       ly_ref.at[blk_slot, pl.ds(local_row, rows)], sem.at[blk_slot])
            cp.start() if act == "start" else cp.wait()
        _for_each_strip_piece(cnt_ref, dst_ref, blk, tb, piece)

    @pl.when(i == 0)
    def _():
        ly_ref[...] = jnp.zeros_like(ly_ref)
        strips(0, 0, "start")

    @pl.when(i + 1 < pl.num_programs(0))
    def _():
        strips(i + 1, 1 - slot, "start")

    strips(i, slot, "wait")

    c = lax.broadcasted_iota(jnp.int32, (tb, n), 1)
    w = jnp.where(c == lpos_ref[:, 0:1], gate_ref[:, 0:1], 0.0)
    for k in range(1, TOP_K):
        w = w + jnp.where(c == lpos_ref[:, k:k + 1], gate_ref[:, k:k + 1], 0.0)
    m = jnp.dot(w.astype(BF16), ly_ref[slot].astype(BF16), preferred_element_type=F32)
    o_ref[...] = _layer_norm(DEEPNORM_ALPHA * x_ref[...] + m, g_ref[...], b_ref[...])


def _combine(x1, lpos_t, gates_t, ln_g, ln_b, cnt_flat, dst_flat, yb):
    t, d = x1.shape
    tb = min(MOE_TOKENS, t)
    return pl.pallas_call(
        _combine_kernel,
        grid_spec=pltpu.PrefetchScalarGridSpec(
            num_scalar_prefetch=2,
            grid=(t // tb,),
            in_specs=[
                pl.BlockSpec((tb, d), lambda i, c, g: (i, 0)),
                pl.BlockSpec((tb, 2 * TOP_K), lambda i, c, g: (i, 0)),
                pl.BlockSpec((tb, 2 * TOP_K), lambda i, c, g: (i, 0)),
                pl.BlockSpec((1, d), lambda i, c, g: (0, 0)),
                pl.BlockSpec((1, d), lambda i, c, g: (0, 0)),
                pl.BlockSpec(memory_space=pl.ANY),
            ],
            out_specs=pl.BlockSpec((tb, d), lambda i, c, g: (i, 0)),
            scratch_shapes=[
                pltpu.VMEM((2, _local_rows(tb), d), F32),
                pltpu.SemaphoreType.DMA((2,)),
            ],
        ),
        out_shape=jax.ShapeDtypeStruct((t, d), F32),
        compiler_params=_params("arbitrary"),
        name="moe_combine",
    )(cnt_flat, dst_flat, x1, lpos_t, gates_t, ln_g, ln_b, yb)


def _moe(x1, logits, ln_g, ln_b, w_gu, b_gu, w_down, b_down, layer):
    t, d = x1.shape
    e = N_EXPERTS
    bm = EXPERT_ROWS
    lpos, gates, cnt_blk, before_blk = _route(logits[:, :e].T)
    cnt_blk = cnt_blk[:, :, 0].astype(jnp.int32)
    before_blk = before_blk[:, :, 0].astype(jnp.int32)
    cnt = before_blk[-1] + cnt_blk[-1]
    padded = (cnt + bm - 1) // bm * bm
    pad_end = jnp.cumsum(padded)
    pad_start = pad_end - padded
    dst_blk = pad_start[None, :] + before_blk
    nb = -(-(t * TOP_K + (STRIP_ALIGN - 1) * cnt_blk.shape[0] * e) // bm) + e
    n_used = (pad_end[-1] // bm).astype(jnp.int32)
    blk_start = jnp.arange(nb, dtype=jnp.int32) * bm
    blk_e = jnp.minimum(jnp.sum(pad_end[None, :] <= blk_start[:, None], axis=1), e - 1).astype(jnp.int32)
    blk_e = jnp.where(jnp.arange(nb) < n_used, blk_e, blk_e[jnp.maximum(n_used - 1, 0)])
    tails = jnp.concatenate([jnp.where(padded > 0, pad_end - bm, -1), n_used.reshape(1)]).astype(jnp.int32)
    cnt_flat = cnt_blk.reshape(-1)
    dst_flat = dst_blk.reshape(-1).astype(jnp.int32)
    xb = _dispatch(x1, lpos, cnt_flat, dst_flat, tails, nb * bm)
    yb = _expert_ffn(xb, blk_e, n_used.reshape(1), w_gu, b_gu, w_down, b_down, layer)
    return _combine(x1, lpos.T, gates.T, ln_g, ln_b, cnt_flat, dst_flat, yb)


def _qkv_rope_kernel(x_ref, w_ref, pos_ref, inv_ref, out_ref):
    xb = x_ref[...].astype(BF16)
    tm = x_ref.shape[0]
    n_rope = 2 * (2 * DIFF_HEADS * DIFF_HD)
    ang = pos_ref[...].astype(F32) * inv_ref[...]
    cos = jnp.cos(ang)
    sin = jnp.sin(ang)
    lane = lax.broadcasted_iota(jnp.int32, (tm, LANES), 1)
    first_half = (lane % DIFF_HD) < (DIFF_HD // 2)
    sin_signed = jnp.where(first_half, -sin, sin)
    q_scale = DIFF_HD ** -0.5 * math.log2(math.e)
    n = w_ref.shape[1]
    for j in range(0, n, LANES):
        c = jnp.dot(xb, w_ref[:, j:j + LANES], preferred_element_type=F32)
        if j < n_rope:
            rot = jnp.where(first_half, pltpu.roll(c, LANES - DIFF_HD // 2, 1), pltpu.roll(c, DIFF_HD // 2, 1))
            c = c * cos + rot * sin_signed
            if j < n_rope // 2:
                c = c * q_scale
        out_ref[:, j:j + LANES] = c.astype(BF16)


def _qkv_rope(x2d, w_qkv, pos_col, inv_row):
    t, d = x2d.shape
    n = w_qkv.shape[1]
    tm = min(ROW_TILE, t)
    return pl.pallas_call(
        _qkv_rope_kernel,
        grid=(t // tm,),
        in_specs=[
            pl.BlockSpec((tm, d), lambda i: (i, 0)),
            pl.BlockSpec((d, n), lambda i: (0, 0)),
            pl.BlockSpec((tm, 1), lambda i: (i, 0)),
            pl.BlockSpec((1, LANES), lambda i: (0, 0)),
        ],
        out_specs=pl.BlockSpec((tm, n), lambda i: (i, 0)),
        out_shape=jax.ShapeDtypeStruct((t, n), BF16),
        compiler_params=_params("arbitrary"),
        name="qkv_rope",
    )(x2d, w_qkv, pos_col, inv_row)


def _diff_attn_kernel(q_ref, k_ref, v_ref, lam_ref, sw_ref, o_ref, qs_ref, m_ref, l_ref, acc_ref, *, lambda_init):
    qi = pl.program_id(2)
    tq = q_ref.shape[0]
    tk = ATTN_K if k_ref.shape[0] >= ATTN_K else k_ref.shape[0]
    q = q_ref[...]
    lane = lax.broadcasted_iota(jnp.int32, q.shape, 1)
    zero = jnp.zeros_like(q)
    qs_ref[0:tq, :] = jnp.where(lane < DIFF_HD, q, zero)
    qs_ref[tq:2 * tq, :] = jnp.where(lane >= DIFF_HD, q, zero)
    m_ref[...] = jnp.full(m_ref.shape, -jnp.inf, F32)
    l_ref[...] = jnp.zeros_like(l_ref)
    acc_ref[...] = jnp.zeros_like(acc_ref)

    def block(kj, masked):
        k0 = pl.multiple_of(kj * tk, tk)
        s = _nt_dot(qs_ref[...], k_ref[pl.ds(k0, tk), :])
        if masked:
            qpos = qi * tq + lax.broadcasted_iota(jnp.int32, (2 * tq, tk), 0) % tq
            kpos = k0 + lax.broadcasted_iota(jnp.int32, (2 * tq, tk), 1)
            s = jnp.where(kpos <= qpos, s, -jnp.inf)
        m_old = m_ref[...]
        m_new = jnp.maximum(m_old, jnp.max(s, axis=-1, keepdims=True))
        p = jnp.exp2((s - jnp.concatenate([m_new] * (tk // LANES), axis=1)).astype(BF16))
        alpha = jnp.exp2(m_old - m_new)
        l_ref[...] = alpha * l_ref[...] + jnp.sum(p.astype(F32), axis=-1, keepdims=True)
        acc_ref[...] = alpha * acc_ref[...] + jnp.dot(p, v_ref[pl.ds(k0, tk), :], preferred_element_type=F32)
        m_ref[...] = m_new

    n_full = (qi * tq) // tk

    def body(kj, c):
        block(kj, False)
        return c

    lax.fori_loop(0, n_full, body, 0)
    for d in range(tq // tk):
        block(n_full + d, True)

    lam = lam_ref[...]
    lam_full = (jnp.exp(jnp.sum(lam[0:1] * lam[1:2], axis=-1, keepdims=True))
                - jnp.exp(jnp.sum(lam[2:3] * lam[3:4], axis=-1, keepdims=True)) + lambda_init)
    o1 = acc_ref[0:tq, :] / l_ref[0:tq, :]
    o2 = acc_ref[tq:2 * tq, :] / l_ref[tq:2 * tq, :]
    o = o1 - lam_full * o2
    o = o * lax.rsqrt(jnp.mean(o * o, axis=-1, keepdims=True) + RMS_EPS) * sw_ref[...]
    o_ref[...] = (o * (1.0 - lambda_init)).astype(BF16)


def _diff_attn(qkv, lam, subln_w, b, s, lambda_init):
    t = b * s
    h, w = DIFF_HEADS, 2 * DIFF_HD
    tq = min(ATTN_Q, s)
    nq = s // tq
    return pl.pallas_call(
        functools.partial(_diff_attn_kernel, lambda_init=lambda_init),
        grid=(b, h, nq),
        in_specs=[
            pl.BlockSpec((tq, w), lambda bi, hi, qi: (bi * nq + qi, hi)),
            pl.BlockSpec((s, w), lambda bi, hi, qi: (bi, h + hi)),
            pl.BlockSpec((s, DIFF_VD), lambda bi, hi, qi: (bi, 2 * h + hi)),
            pl.BlockSpec((4, DIFF_HD), lambda bi, hi, qi: (0, 0)),
            pl.BlockSpec((1, DIFF_VD), lambda bi, hi, qi: (0, 0)),
        ],
        out_specs=pl.BlockSpec((tq, DIFF_VD), lambda bi, hi, qi: (bi * nq + qi, hi)),
        out_shape=jax.ShapeDtypeStruct((t, h * DIFF_VD), BF16),
        scratch_shapes=[
            pltpu.VMEM((2 * tq, w), BF16),
            pltpu.VMEM((2 * tq, LANES), F32),
            pltpu.VMEM((2 * tq, LANES), F32),
            pltpu.VMEM((2 * tq, DIFF_VD), F32),
        ],
        compiler_params=_params("arbitrary", "arbitrary", "arbitrary"),
        name="diff_attn",
    )(qkv, qkv, qkv, lam, subln_w)


def _pad_cols(a, n):
    return jnp.pad(a, ((0, 0), (0, n - a.shape[1])))


def kernel(x, positions, a_w_in, a_w_gk2, a_b_gk, a_norm_w, a_w_out, kv_w, b_w_q, b_lambda, b_subln_w,
           b_w_out, ln1_g, ln1_b, ln2_g, ln2_b, router_w, router_b, moe_w_gu, moe_b_gu, moe_w_down,
           moe_b_down):
    b, s, d = x.shape
    t = b * s
    xc = x.reshape(t, d)
    n_main = 2 * GLA_HEADS * GLA_DK + 2 * GLA_HEADS * GLA_DV
    half = DIFF_HD // 2
    inv = ROPE_THETA ** (-jnp.arange(half, dtype=F32) * 2.0 / DIFF_HD)
    inv_row = jnp.tile(inv, LANES // half).reshape(1, LANES)
    pos_col = positions.reshape(t, 1)
    assert DEPTH - N_A_LAYERS == 1
    for l in range(DEPTH):
        if l < N_A_LAYERS:
            w_in = a_w_in[l]
            qkvr, logg = _gla_proj(
                xc, w_in[:, :n_main].astype(BF16), _pad_cols(w_in[:, n_main:], LANES).astype(BF16),
                jnp.pad(a_w_gk2[l], ((0, LANES - GLA_GATE_RANK), (0, 0))), a_b_gk[l].reshape(1, -1))
            o = _gla_core(qkvr, logg, a_norm_w[l].reshape(1, -1), b, s)
            w_out = a_w_out[l]
        else:
            j = l - N_A_LAYERS
            lambda_init = 0.8 - 0.6 * math.exp(-0.3 * l)
            w_qkv = jnp.concatenate([b_w_q[j], kv_w], axis=1).astype(BF16)
            qkv = _qkv_rope(xc, w_qkv, pos_col, inv_row)
            o = _diff_attn(qkv, b_lambda[j], b_subln_w[j].reshape(1, -1), b, s, lambda_init)
            w_out = b_w_out[j]
        rw = _pad_cols(router_w[l], LANES)
        rw_hi = rw.astype(BF16)
        rw_lo = (rw - rw_hi.astype(F32)).astype(BF16)
        x1, logits = _proj_ln(o, w_out.astype(BF16), xc, ln1_g[l].reshape(1, -1), ln1_b[l].reshape(1, -1),
                              jnp.concatenate([rw_hi, rw_lo], axis=1), _pad_cols(router_b[l].reshape(1, -1), LANES))
        xc = _moe(x1, logits, ln2_g[l].reshape(1, -1), ln2_b[l].reshape(1, -1),
                  moe_w_gu, moe_b_gu, moe_w_down, moe_b_down, l)
    return xc.reshape(b, s, d)
```

```python
import functools
import math

import jax
import jax.numpy as jnp
from jax import lax
from jax.experimental import pallas as pl
from jax.experimental.pallas import tpu as pltpu

F32 = jnp.float32
BF16 = jnp.bfloat16
HIGHEST = lax.Precision.HIGHEST

DEPTH = 2
N_A_LAYERS = DEPTH // 2
GLA_HEADS = 4
GLA_DK = 128
GLA_DV = 256
GLA_GATE_RANK = 16
GLA_GATE_TAU = 16.0
GLA_CHUNK = 64
DIFF_HEADS = 8
DIFF_HD = 64
DIFF_VD = 128
ROPE_THETA = 10000.0
N_EXPERTS = 32
TOP_K = 4
SWIGLU_LIMIT = 7.0
SWIGLU_ALPHA = 1.702
LN_EPS = 1e-5
RMS_EPS = 1e-5
DEEPNORM_ALPHA = (2.0 * DEPTH) ** 0.25

LANES = 128
SUBLANES = 8
MXU_COLS = 256
BF16_EXACT_INT = 256
STRIP_ALIGN = SUBLANES
VMEM_LIMIT_BYTES = 56 * 1024 * 1024

ROW_TILE = 512
EXPERT_ROWS = 512
EXPERT_FC = 512
MOE_TOKENS = 512
ATTN_Q = 512
ATTN_K = 512
GLA_ROWS = 512


def _params(*sem):
    return pltpu.CompilerParams(dimension_semantics=sem, vmem_limit_bytes=VMEM_LIMIT_BYTES)


def _nt_dot(a, b):
    return lax.dot_general(a, b, (((1,), (1,)), ((), ())), preferred_element_type=F32)


def _gla_proj_kernel(x_ref, w_ref, wg_ref, wgk2_ref, bgk_ref, qkvr_ref, logg_ref):
    xb = x_ref[...].astype(BF16)
    n = w_ref.shape[1]
    for j in range(0, n, 512):
        qkvr_ref[:, j:j + 512] = jnp.dot(xb, w_ref[:, j:j + 512],
                                         preferred_element_type=F32).astype(BF16)
    gk_low = jnp.dot(xb, wg_ref[...], preferred_element_type=F32)
    z = jnp.dot(gk_low, wgk2_ref[...], preferred_element_type=F32, precision=HIGHEST) + bgk_ref[...]
    logg_ref[...] = (jnp.minimum(z, 0.0) - jnp.log1p(jnp.exp(-jnp.abs(z)))) / GLA_GATE_TAU


def _gla_proj(x2d, w_main, w_gate, w_gk2, b_gk):
    t, d = x2d.shape
    n = w_main.shape[1]
    hk = b_gk.shape[1]
    tm = min(ROW_TILE, t)
    return pl.pallas_call(
        _gla_proj_kernel,
        grid=(t // tm,),
        in_specs=[
            pl.BlockSpec((tm, d), lambda i: (i, 0)),
            pl.BlockSpec((d, n), lambda i: (0, 0)),
            pl.BlockSpec((d, LANES), lambda i: (0, 0)),
            pl.BlockSpec((LANES, hk), lambda i: (0, 0)),
            pl.BlockSpec((1, hk), lambda i: (0, 0)),
        ],
        out_specs=[
            pl.BlockSpec((tm, n), lambda i: (i, 0)),
            pl.BlockSpec((tm, hk), lambda i: (i, 0)),
        ],
        out_shape=[jax.ShapeDtypeStruct((t, n), BF16), jax.ShapeDtypeStruct((t, hk), F32)],
        compiler_params=_params("arbitrary"),
        name="gla_proj",
    )(x2d, w_main, w_gate, w_gk2, b_gk)


def _gla_kernel(q_ref, k_ref, v_ref, r_ref, lg_ref, nw_ref, o_ref, st_ref):
    c = GLA_CHUNK

    @pl.when(pl.program_id(2) == 0)
    def _():
        st_ref[...] = jnp.zeros_like(st_ref)

    row = lax.broadcasted_iota(jnp.int32, (c, c), 0)
    col = lax.broadcasted_iota(jnp.int32, (c, c), 1)
    tril = row >= col
    row_id = lax.broadcasted_iota(jnp.int32, (c, GLA_DK), 0)
    n_chunks = q_ref.shape[0] // c

    st = st_ref[...]
    for ci in range(n_chunks):
        r0 = ci * c
        lg = lg_ref[pl.ds(r0, c), :]
        g = lg
        shift = 1
        while shift < c:
            g = g + jnp.where(row_id >= shift, pltpu.roll(g, shift, 0), 0.0)
            shift *= 2
        g_last = g[c - 1:c, :]
        q = q_ref[pl.ds(r0, c), :].astype(F32) * (GLA_DK ** -0.5)
        k = k_ref[pl.ds(r0, c), :].astype(F32)
        v = v_ref[pl.ds(r0, c), :]
        q_in = (q * jnp.exp(g)).astype(BF16)
        k_in = (k * jnp.exp(-g)).astype(BF16)
        k_state = (k * jnp.exp(g_last - g)).astype(BF16)
        decay = jnp.exp(g_last)
        scores = jnp.where(tril, _nt_dot(q_in, k_in), 0.0).astype(BF16)
        o = jnp.dot(scores, v, preferred_element_type=F32) + _nt_dot(q_in, st.astype(BF16))
        v_t = v.astype(F32).T.astype(BF16)
        st = st * decay + jnp.dot(v_t, k_state, preferred_element_type=F32)
        o = o * lax.rsqrt(jnp.mean(o * o, axis=-1, keepdims=True) + RMS_EPS) * nw_ref[...]
        r = r_ref[pl.ds(r0, c), :].astype(F32)
        o_ref[pl.ds(r0, c), :] = (o * (r * jax.nn.sigmoid(r))).astype(BF16)
    st_ref[...] = st


def _gla_core(qkvr, logg, norm_w, b, s):
    t = b * s
    h, dk, dv = GLA_HEADS, GLA_DK, GLA_DV
    ts = min(GLA_ROWS, s)
    ns = s // ts
    k_off = (h * dk) // dk
    v_off = (2 * h * dk) // dv
    r_off = (2 * h * dk + h * dv) // dv
    return pl.pallas_call(
        _gla_kernel,
        grid=(b, h, ns),
        in_specs=[
            pl.BlockSpec((ts, dk), lambda bi, hi, si: (bi * ns + si, hi)),
            pl.BlockSpec((ts, dk), lambda bi, hi, si: (bi * ns + si, k_off + hi)),
            pl.BlockSpec((ts, dv), lambda bi, hi, si: (bi * ns + si, v_off + hi)),
            pl.BlockSpec((ts, dv), lambda bi, hi, si: (bi * ns + si, r_off + hi)),
            pl.BlockSpec((ts, dk), lambda bi, hi, si: (bi * ns + si, hi)),
            pl.BlockSpec((1, dv), lambda bi, hi, si: (0, 0)),
        ],
        out_specs=pl.BlockSpec((ts, dv), lambda bi, hi, si: (bi * ns + si, hi)),
        out_shape=jax.ShapeDtypeStruct((t, h * dv), BF16),
        scratch_shapes=[pltpu.VMEM((dv, dk), F32)],
        compiler_params=_params("arbitrary", "arbitrary", "arbitrary"),
        name="gla_core",
    )(qkvr, qkvr, qkvr, qkvr, logg, norm_w)


def _layer_norm(y, g, b):
    mu = jnp.mean(y, axis=-1, keepdims=True)
    yc = y - mu
    var = jnp.mean(yc * yc, axis=-1, keepdims=True)
    return yc * lax.rsqrt(var + LN_EPS) * g + b


def _proj_ln_kernel(o_ref, w_ref, x_ref, g_ref, b_ref, rw_ref, rb_ref, x1_ref, logit_ref):
    h = jnp.dot(o_ref[...], w_ref[...], preferred_element_type=F32)
    x1 = _layer_norm(DEEPNORM_ALPHA * x_ref[...] + h, g_ref[...], b_ref[...])
    x1_ref[...] = x1
    xh = x1.astype(BF16)
    xl = (x1 - xh.astype(F32)).astype(BF16)
    hw = jnp.dot(xh, rw_ref[...], preferred_element_type=F32)
    lw = jnp.dot(xl, rw_ref[:, 0:LANES], preferred_element_type=F32)
    logit_ref[...] = hw[:, 0:LANES] + hw[:, LANES:2 * LANES] + lw + rb_ref[...]


def _proj_ln(o, w_out, x2d, ln_g, ln_b, rw, rb):
    t, d = x2d.shape
    kdim = o.shape[1]
    tm = min(ROW_TILE, t)
    return pl.pallas_call(
        _proj_ln_kernel,
        grid=(t // tm,),
        in_specs=[
            pl.BlockSpec((tm, kdim), lambda i: (i, 0)),
            pl.BlockSpec((kdim, d), lambda i: (0, 0)),
            pl.BlockSpec((tm, d), lambda i: (i, 0)),
            pl.BlockSpec((1, d), lambda i: (0, 0)),
            pl.BlockSpec((1, d), lambda i: (0, 0)),
            pl.BlockSpec((d, 2 * LANES), lambda i: (0, 0)),
            pl.BlockSpec((1, LANES), lambda i: (0, 0)),
        ],
        out_specs=[
            pl.BlockSpec((tm, d), lambda i: (i, 0)),
            pl.BlockSpec((tm, LANES), lambda i: (i, 0)),
        ],
        out_shape=[jax.ShapeDtypeStruct((t, d), F32), jax.ShapeDtypeStruct((t, LANES), F32)],
        compiler_params=_params("arbitrary"),
        name="proj_ln",
    )(o, w_out, x2d, ln_g, ln_b, rw, rb)


def _route_kernel(lt_ref, lpos_ref, gate_ref, cnt_ref, before_ref, carry_ref):
    e, tb = lt_ref.shape

    @pl.when(pl.program_id(0) == 0)
    def _():
        carry_ref[...] = jnp.zeros_like(carry_ref)

    vals = lt_ref[...]
    eidx = lax.broadcasted_iota(jnp.int32, (e, tb), 0).astype(F32)
    top_v, hots = [], []
    for _ in range(TOP_K):
        m = jnp.max(vals, axis=0, keepdims=True)
        idx = jnp.min(jnp.where(vals == m, eidx, float(e)), axis=0, keepdims=True)
        hot = eidx == idx
        vals = jnp.where(hot, -jnp.inf, vals)
        top_v.append(m)
        hots.append(hot)
    ex = [jnp.exp(v - top_v[0]) for v in top_v]
    den = ex[0] + ex[1] + ex[2] + ex[3]
    gates = [x / den for x in ex]
    multi = jnp.zeros((e, tb), F32)
    for hot in hots:
        multi = multi + jnp.where(hot, 1.0, 0.0)
    ri = lax.broadcasted_iota(jnp.int32, (tb, tb), 0)
    ci = lax.broadcasted_iota(jnp.int32, (tb, tb), 1)
    upper = jnp.where(ri <= ci, 1.0, 0.0).astype(BF16)
    incl = jnp.dot(multi.astype(BF16), upper, preferred_element_type=F32)
    cnt = jnp.broadcast_to(incl[:, tb - 1:tb], (e, LANES))
    cnt = jnp.ceil(cnt * (1.0 / STRIP_ALIGN)) * STRIP_ALIGN
    er = lax.broadcasted_iota(jnp.int32, (e, e), 0)
    ec = lax.broadcasted_iota(jnp.int32, (e, e), 1)
    lower = jnp.where(er > ec, 1.0, 0.0).astype(BF16)
    cnt_hi = jnp.floor(cnt * (1.0 / BF16_EXACT_INT)) * BF16_EXACT_INT
    cnt_lo = cnt - cnt_hi
    start = (jnp.dot(lower, cnt_hi.astype(BF16), preferred_element_type=F32)
             + jnp.dot(lower, cnt_lo.astype(BF16), preferred_element_type=F32))[:, 0:1]
    where_to = start + incl - multi
    lpos = [jnp.sum(jnp.where(hot, where_to, 0.0), axis=0, keepdims=True) for hot in hots]
    lpos_ref[...] = jnp.concatenate(lpos + lpos, axis=0).astype(jnp.int32)
    gate_ref[...] = jnp.concatenate(gates + [jnp.zeros_like(g) for g in gates], axis=0)
    cnt_ref[...] = cnt
    before_ref[...] = carry_ref[...]
    carry_ref[...] = carry_ref[...] + cnt


def _route(logits_t):
    e, t = logits_t.shape
    tb = min(MOE_TOKENS, t)
    nblk = t // tb
    return pl.pallas_call(
        _route_kernel,
        grid=(nblk,),
        in_specs=[pl.BlockSpec((e, tb), lambda i: (0, i))],
        out_specs=[
            pl.BlockSpec((2 * TOP_K, tb), lambda i: (0, i)),
            pl.BlockSpec((2 * TOP_K, tb), lambda i: (0, i)),
            pl.BlockSpec((None, e, LANES), lambda i: (i, 0, 0)),
            pl.BlockSpec((None, e, LANES), lambda i: (i, 0, 0)),
        ],
        out_shape=[
            jax.ShapeDtypeStruct((2 * TOP_K, t), jnp.int32),
            jax.ShapeDtypeStruct((2 * TOP_K, t), F32),
            jax.ShapeDtypeStruct((nblk, e, LANES), F32),
            jax.ShapeDtypeStruct((nblk, e, LANES), F32),
        ],
        scratch_shapes=[pltpu.VMEM((e, LANES), F32)],
        compiler_params=_params("arbitrary"),
        name="route",
    )(logits_t)


def _local_rows(tb):
    return TOP_K * tb + STRIP_ALIGN * N_EXPERTS


def _for_each_strip(cnt_ref, dst_ref, blk, fn):
    def per_expert(e, local_row):
        c = cnt_ref[blk * N_EXPERTS + e]
        g = dst_ref[blk * N_EXPERTS + e]

        @pl.when(c > 0)
        def _():
            fn(pl.multiple_of(local_row, STRIP_ALIGN), pl.multiple_of(g, STRIP_ALIGN), pl.multiple_of(c, STRIP_ALIGN))

        return local_row + c

    lax.fori_loop(0, N_EXPERTS, per_expert, jnp.int32(0))


def _dispatch_kernel(cnt_ref, dst_ref, tail_ref, x_ref, lpos_ref, xb_hbm, ls_ref, zero_ref, sem, zero_sem):
    i = pl.program_id(0)
    tb = x_ref.shape[0]
    n = ls_ref.shape[1]
    bm = zero_ref.shape[0]

    @pl.when(i == 0)
    def _():
        zero_ref[...] = jnp.zeros_like(zero_ref)
        n_blocks = xb_hbm.shape[0] // bm
        n_used = tail_ref[N_EXPERTS]

        def zero_copy(row):
            return pltpu.make_async_copy(zero_ref, xb_hbm.at[pl.ds(pl.multiple_of(row, STRIP_ALIGN), bm)], zero_sem)

        def start(e, c):
            @pl.when(tail_ref[e] >= 0)
            def _():
                zero_copy(tail_ref[e]).start()
            return c

        def wait(e, c):
            @pl.when(tail_ref[e] >= 0)
            def _():
                zero_copy(tail_ref[e]).wait()
            return c

        def start_unused(blk, c):
            zero_copy(blk * bm).start()
            return c

        def wait_unused(blk, c):
            zero_copy(blk * bm).wait()
            return c

        lax.fori_loop(0, N_EXPERTS, start, 0)
        lax.fori_loop(n_used, n_blocks, start_unused, 0)
        lax.fori_loop(0, N_EXPERTS, wait, 0)
        lax.fori_loop(n_used, n_blocks, wait_unused, 0)

    r = lax.broadcasted_iota(jnp.int32, (n, tb), 0)
    lp = lpos_ref[...]
    perm = jnp.zeros((n, tb), F32)
    for k in range(TOP_K):
        perm = jnp.where(r == lp[k:k + 1], 1.0, perm)
    slot = i % 2
    ls_ref[slot] = jnp.dot(perm.astype(BF16), x_ref[...].astype(BF16), preferred_element_type=F32)

    def strips(blk, blk_slot, act):
        def piece(local_row, global_row, rows):
            cp = pltpu.make_async_copy(ls_ref.at[blk_slot, pl.ds(local_row, rows)],
                                       xb_hbm.at[pl.ds(global_row, rows)], sem.at[blk_slot])
            cp.start() if act == "start" else cp.wait()
        _for_each_strip(cnt_ref, dst_ref, blk, piece)

    strips(i, slot, "start")

    @pl.when(i > 0)
    def _():
        strips(i - 1, 1 - slot, "wait")

    @pl.when(i == pl.num_programs(0) - 1)
    def _():
        strips(i, slot, "wait")


def _dispatch(x1, lpos, cnt_flat, dst_flat, tails, n_pad):
    t, d = x1.shape
    tb = min(MOE_TOKENS, t)
    return pl.pallas_call(
        _dispatch_kernel,
        grid_spec=pltpu.PrefetchScalarGridSpec(
            num_scalar_prefetch=3,
            grid=(t // tb,),
            in_specs=[
                pl.BlockSpec((tb, d), lambda i, c, g, tl: (i, 0)),
                pl.BlockSpec((2 * TOP_K, tb), lambda i, c, g, tl: (0, i)),
            ],
            out_specs=pl.BlockSpec(memory_space=pl.ANY),
            scratch_shapes=[
                pltpu.VMEM((2, _local_rows(tb), d), F32),
                pltpu.VMEM((EXPERT_ROWS, d), F32),
                pltpu.SemaphoreType.DMA((2,)),
                pltpu.SemaphoreType.DMA,
            ],
        ),
        out_shape=jax.ShapeDtypeStruct((n_pad, d), F32),
        compiler_params=_params("arbitrary"),
        name="moe_dispatch",
    )(cnt_flat, dst_flat, tails, x1, lpos)


def _expert_kernel(be_ref, nu_ref, x_ref, wgu_ref, bgu_ref, wd_ref, bd_ref, y_ref, wgu_b, wd_b):
    i = pl.program_id(0)
    f = wd_ref.shape[0]

    @pl.when(i >= nu_ref[0])
    def _():
        y_ref[...] = jnp.zeros_like(y_ref)

    @pl.when(i < nu_ref[0])
    def _():
        prev = be_ref[jnp.maximum(i - 1, 0)]

        @pl.when((i == 0) | (be_ref[i] != prev))
        def _():
            wgu_b[...] = wgu_ref[...].astype(BF16)
            wd_b[...] = wd_ref[...].astype(BF16)

        xb = x_ref[...].astype(BF16)
        fc = EXPERT_FC
        acc = jnp.zeros(y_ref.shape, F32) + bd_ref[...]
        for j in range(0, f, fc):
            h_glu = jnp.dot(xb, wgu_b[:, j:j + fc], preferred_element_type=F32) + bgu_ref[:, j:j + fc]
            h_lin = (jnp.dot(xb, wgu_b[:, f + j:f + j + fc], preferred_element_type=F32)
                     + bgu_ref[:, f + j:f + j + fc])
            x_glu = jnp.minimum(h_glu, SWIGLU_LIMIT)
            x_lin = jnp.clip(h_lin, -SWIGLU_LIMIT, SWIGLU_LIMIT)
            act = x_glu * jax.nn.sigmoid(SWIGLU_ALPHA * x_glu) * (x_lin + 1.0)
            acc = acc + jnp.dot(act.astype(BF16), wd_b[j:j + fc, :], preferred_element_type=F32)
        y_ref[...] = acc


def _expert_ffn(xb, blk_e, n_used, w_gu, b_gu, w_down, b_down, layer):
    n_pad, d = xb.shape
    _, e, _, f2 = w_gu.shape
    f = f2 // 2
    bm = EXPERT_ROWS
    nb = n_pad // bm
    b_gu4 = b_gu.reshape(b_gu.shape[0], e, 1, f2)
    b_down4 = b_down.reshape(b_down.shape[0], e, 1, d)

    def row_map(i, be, nu):
        return (jnp.minimum(i, nu[0] - 1), 0)

    def w_map(i, be, nu):
        return (layer, be[i], 0, 0)

    return pl.pallas_call(
        _expert_kernel,
        grid_spec=pltpu.PrefetchScalarGridSpec(
            num_scalar_prefetch=2,
            grid=(nb,),
            in_specs=[
                pl.BlockSpec((bm, d), row_map),
                pl.BlockSpec((None, None, d, f2), w_map),
                pl.BlockSpec((None, None, 1, f2), w_map),
                pl.BlockSpec((None, None, f, d), w_map),
                pl.BlockSpec((None, None, 1, d), w_map),
            ],
            out_specs=pl.BlockSpec((bm, d), lambda i, be, nu: (i, 0)),
            scratch_shapes=[pltpu.VMEM((d, f2), BF16), pltpu.VMEM((f, d), BF16)],
        ),
        out_shape=jax.ShapeDtypeStruct((n_pad, d), F32),
        compiler_params=_params("arbitrary"),
        name="moe_experts",
    )(blk_e, n_used, xb, w_gu, b_gu4, w_down, b_down4)


def _combine_kernel(cnt_ref, dst_ref, x_ref, lpos_ref, gate_ref, g_ref, b_ref, yb_hbm, o_ref, ly_ref, sem):
    i = pl.program_id(0)
    tb = x_ref.shape[0]
    n = ly_ref.shape[1]
    slot = i % 2

    def strips(blk, blk_slot, act):
        def piece(local_row, global_row, rows):
            cp = pltpu.make_async_copy(yb_hbm.at[pl.ds(global_row, rows)],
                                       ly_ref.at[blk_slot, pl.ds(local_row, rows)], sem.at[blk_slot])
            cp.start() if act == "start" else cp.wait()
        _for_each_strip(cnt_ref, dst_ref, blk, piece)

    @pl.when(i == 0)
    def _():
        ly_ref[...] = jnp.zeros_like(ly_ref)
        strips(0, 0, "start")

    @pl.when(i + 1 < pl.num_programs(0))
    def _():
        strips(i + 1, 1 - slot, "start")

    strips(i, slot, "wait")

    c = lax.broadcasted_iota(jnp.int32, (tb, n), 1)
    w = jnp.zeros((tb, n), F32)
    for k in range(TOP_K):
        w = jnp.where(c == lpos_ref[:, k:k + 1], gate_ref[:, k:k + 1], w)
    m = jnp.dot(w.astype(BF16), ly_ref[slot].astype(BF16), preferred_element_type=F32)
    o_ref[...] = _layer_norm(DEEPNORM_ALPHA * x_ref[...] + m, g_ref[...], b_ref[...])


def _combine(x1, lpos_t, gates_t, ln_g, ln_b, cnt_flat, dst_flat, yb):
    t, d = x1.shape
    tb = min(MOE_TOKENS, t)
    return pl.pallas_call(
        _combine_kernel,
        grid_spec=pltpu.PrefetchScalarGridSpec(
            num_scalar_prefetch=2,
            grid=(t // tb,),
            in_specs=[
                pl.BlockSpec((tb, d), lambda i, c, g: (i, 0)),
                pl.BlockSpec((tb, 2 * TOP_K), lambda i, c, g: (i, 0)),
                pl.BlockSpec((tb, 2 * TOP_K), lambda i, c, g: (i, 0)),
                pl.BlockSpec((1, d), lambda i, c, g: (0, 0)),
                pl.BlockSpec((1, d), lambda i, c, g: (0, 0)),
                pl.BlockSpec(memory_space=pl.ANY),
            ],
            out_specs=pl.BlockSpec((tb, d), lambda i, c, g: (i, 0)),
            scratch_shapes=[
                pltpu.VMEM((2, _local_rows(tb), d), F32),
                pltpu.SemaphoreType.DMA((2,)),
            ],
        ),
        out_shape=jax.ShapeDtypeStruct((t, d), F32),
        compiler_params=_params("arbitrary"),
        name="moe_combine",
    )(cnt_flat, dst_flat, x1, lpos_t, gates_t, ln_g, ln_b, yb)


def _moe(x1, logits, ln_g, ln_b, w_gu, b_gu, w_down, b_down, layer):
    t, d = x1.shape
    e = N_EXPERTS
    bm = EXPERT_ROWS
    lpos, gates, cnt_blk, before_blk = _route(logits[:, :e].T)
    cnt_blk = cnt_blk[:, :, 0].astype(jnp.int32)
    before_blk = before_blk[:, :, 0].astype(jnp.int32)
    cnt = before_blk[-1] + cnt_blk[-1]
    padded = (cnt + bm - 1) // bm * bm
    pad_end = jnp.cumsum(padded)
    pad_start = pad_end - padded
    dst_blk = pad_start[None, :] + before_blk
    nb = -(-(t * TOP_K + (STRIP_ALIGN - 1) * cnt_blk.shape[0] * e) // bm) + e
    n_used = (pad_end[-1] // bm).astype(jnp.int32)
    blk_start = jnp.arange(nb, dtype=jnp.int32) * bm
    blk_e = jnp.minimum(jnp.sum(pad_end[None, :] <= blk_start[:, None], axis=1), e - 1).astype(jnp.int32)
    blk_e = jnp.where(jnp.arange(nb) < n_used, blk_e, blk_e[jnp.maximum(n_used - 1, 0)])
    tails = jnp.concatenate([jnp.where(padded > 0, pad_end - bm, -1), n_used.reshape(1)]).astype(jnp.int32)
    cnt_flat = cnt_blk.reshape(-1)
    dst_flat = dst_blk.reshape(-1).astype(jnp.int32)
    xb = _dispatch(x1, lpos, cnt_flat, dst_flat, tails, nb * bm)
    yb = _expert_ffn(xb, blk_e, n_used.reshape(1), w_gu, b_gu, w_down, b_down, layer)
    return _combine(x1, lpos.T, gates.T, ln_g, ln_b, cnt_flat, dst_flat, yb)


def _qkv_rope_kernel(x_ref, w_ref, pos_ref, inv_ref, out_ref):
    xb = x_ref[...].astype(BF16)
    tm = x_ref.shape[0]
    n_rope = 2 * (2 * DIFF_HEADS * DIFF_HD)
    n_freq = DIFF_HD // 2
    groups = LANES // n_freq
    rows = tm // groups
    pos = pos_ref[...].astype(F32)
    lane_c = lax.broadcasted_iota(jnp.int32, (rows, LANES), 1)
    pos_c = pos[(groups - 1) * rows:groups * rows]
    for g in reversed(range(groups - 1)):
        pos_c = jnp.where(lane_c < (g + 1) * n_freq, pos[g * rows:(g + 1) * rows], pos_c)
    ang = pos_c * inv_ref[...]

    def spread(tbl):
        out = []
        for g in range(groups):
            y = pltpu.roll(tbl, LANES - g * n_freq, 1) if g else tbl
            width = n_freq
            while width < LANES:
                y = jnp.where(lane_c < width, y, pltpu.roll(y, width, 1))
                width *= 2
            out.append(y)
        return jnp.concatenate(out, axis=0)

    cos = spread(jnp.cos(ang))
    sin = spread(jnp.sin(ang))
    lane = lax.broadcasted_iota(jnp.int32, (tm, LANES), 1)
    first_half = (lane % DIFF_HD) < (DIFF_HD // 2)
    sin_signed = jnp.where(first_half, -sin, sin)
    q_scale = DIFF_HD ** -0.5 * math.log2(math.e)
    n = w_ref.shape[1]
    for j0 in range(0, n, MXU_COLS):
        wide = jnp.dot(xb, w_ref[:, j0:j0 + MXU_COLS], preferred_element_type=F32)
        for j in range(j0, j0 + MXU_COLS, LANES):
            c = wide[:, j - j0:j - j0 + LANES]
            if j < n_rope:
                rot = jnp.where(first_half, pltpu.roll(c, LANES - DIFF_HD // 2, 1), pltpu.roll(c, DIFF_HD // 2, 1))
                c = c * cos + rot * sin_signed
                if j < n_rope // 2:
                    c = c * q_scale
            out_ref[:, j:j + LANES] = c.astype(BF16)


def _qkv_rope(x2d, w_qkv, pos_col, inv_row):
    t, d = x2d.shape
    n = w_qkv.shape[1]
    tm = min(ROW_TILE, t)
    return pl.pallas_call(
        _qkv_rope_kernel,
        grid=(t // tm,),
        in_specs=[
            pl.BlockSpec((tm, d), lambda i: (i, 0)),
            pl.BlockSpec((d, n), lambda i: (0, 0)),
            pl.BlockSpec((tm, 1), lambda i: (i, 0)),
            pl.BlockSpec((1, LANES), lambda i: (0, 0)),
        ],
        out_specs=pl.BlockSpec((tm, n), lambda i: (i, 0)),
        out_shape=jax.ShapeDtypeStruct((t, n), BF16),
        compiler_params=_params("arbitrary"),
        name="qkv_rope",
    )(x2d, w_qkv, pos_col, inv_row)


def _diff_attn_kernel(q_ref, k_ref, v_ref, lam_ref, sw_ref, o_ref, qs_ref, m_ref, l_ref, acc_ref, *, lambda_init):
    qi = pl.program_id(2)
    tq = q_ref.shape[0]
    tk = ATTN_K if k_ref.shape[0] >= ATTN_K else k_ref.shape[0]
    q = q_ref[...]
    lane = lax.broadcasted_iota(jnp.int32, q.shape, 1)
    zero = jnp.zeros_like(q)
    qs_ref[0:tq, :] = jnp.where(lane < DIFF_HD, q, zero)
    qs_ref[tq:2 * tq, :] = jnp.where(lane >= DIFF_HD, q, zero)
    m_ref[...] = jnp.full(m_ref.shape, -jnp.inf, F32)
    l_ref[...] = jnp.zeros_like(l_ref)
    acc_ref[...] = jnp.zeros_like(acc_ref)

    def block(kj, masked):
        k0 = pl.multiple_of(kj * tk, tk)
        s = _nt_dot(qs_ref[...], k_ref[pl.ds(k0, tk), :])
        if masked:
            qpos = qi * tq + lax.broadcasted_iota(jnp.int32, (2 * tq, tk), 0) % tq
            kpos = k0 + lax.broadcasted_iota(jnp.int32, (2 * tq, tk), 1)
            s = jnp.where(kpos <= qpos, s, -jnp.inf)
        m_old = m_ref[...]
        m_new = jnp.maximum(m_old, jnp.max(s, axis=-1, keepdims=True))
        p = jnp.exp2((s - jnp.concatenate([m_new] * (tk // LANES), axis=1)).astype(BF16))
        alpha = jnp.exp2(m_old - m_new)
        l_ref[...] = alpha * l_ref[...] + jnp.sum(p.astype(F32), axis=-1, keepdims=True)
        acc_ref[...] = alpha * acc_ref[...] + jnp.dot(p, v_ref[pl.ds(k0, tk), :], preferred_element_type=F32)
        m_ref[...] = m_new

    n_full = (qi * tq) // tk

    def body(kj, c):
        block(kj, False)
        return c

    lax.fori_loop(0, n_full, body, 0)
    for d in range(tq // tk):
        block(n_full + d, True)

    lam = lam_ref[...]
    lam_full = (jnp.exp(jnp.sum(lam[0:1] * lam[1:2], axis=-1, keepdims=True))
                - jnp.exp(jnp.sum(lam[2:3] * lam[3:4], axis=-1, keepdims=True)) + lambda_init)
    o1 = acc_ref[0:tq, :] / l_ref[0:tq, :]
    o2 = acc_ref[tq:2 * tq, :] / l_ref[tq:2 * tq, :]
    o = o1 - lam_full * o2
    o = o * lax.rsqrt(jnp.mean(o * o, axis=-1, keepdims=True) + RMS_EPS) * sw_ref[...]
    o_ref[...] = (o * (1.0 - lambda_init)).astype(BF16)


def _diff_attn(qkv, lam, subln_w, b, s, lambda_init):
    t = b * s
    h, w = DIFF_HEADS, 2 * DIFF_HD
    tq = min(ATTN_Q, s)
    nq = s // tq
    return pl.pallas_call(
        functools.partial(_diff_attn_kernel, lambda_init=lambda_init),
        grid=(b, h, nq),
        in_specs=[
            pl.BlockSpec((tq, w), lambda bi, hi, qi: (bi * nq + qi, hi)),
            pl.BlockSpec((s, w), lambda bi, hi, qi: (bi, h + hi)),
            pl.BlockSpec((s, DIFF_VD), lambda bi, hi, qi: (bi, 2 * h + hi)),
            pl.BlockSpec((4, DIFF_HD), lambda bi, hi, qi: (0, 0)),
            pl.BlockSpec((1, DIFF_VD), lambda bi, hi, qi: (0, 0)),
        ],
        out_specs=pl.BlockSpec((tq, DIFF_VD), lambda bi, hi, qi: (bi * nq + qi, hi)),
        out_shape=jax.ShapeDtypeStruct((t, h * DIFF_VD), BF16),
        scratch_shapes=[
            pltpu.VMEM((2 * tq, w), BF16),
            pltpu.VMEM((2 * tq, LANES), F32),
            pltpu.VMEM((2 * tq, LANES), F32),
            pltpu.VMEM((2 * tq, DIFF_VD), F32),
        ],
        compiler_params=_params("arbitrary", "arbitrary", "arbitrary"),
        name="diff_attn",
    )(qkv, qkv, qkv, lam, subln_w)


def _pad_cols(a, n):
    return jnp.pad(a, ((0, 0), (0, n - a.shape[1])))


def kernel(x, positions, a_w_in, a_w_gk2, a_b_gk, a_norm_w, a_w_out, kv_w, b_w_q, b_lambda, b_subln_w,
           b_w_out, ln1_g, ln1_b, ln2_g, ln2_b, router_w, router_b, moe_w_gu, moe_b_gu, moe_w_down,
           moe_b_down):
    b, s, d = x.shape
    t = b * s
    xc = x.reshape(t, d)
    n_main = 2 * GLA_HEADS * GLA_DK + 2 * GLA_HEADS * GLA_DV
    half = DIFF_HD // 2
    inv = ROPE_THETA ** (-jnp.arange(half, dtype=F32) * 2.0 / DIFF_HD)
    inv_row = jnp.tile(inv, LANES // half).reshape(1, LANES)
    pos_col = positions.reshape(t, 1)
    assert DEPTH - N_A_LAYERS == 1
    for l in range(DEPTH):
        if l < N_A_LAYERS:
            w_in = a_w_in[l]
            qkvr, logg = _gla_proj(
                xc, w_in[:, :n_main].astype(BF16), _pad_cols(w_in[:, n_main:], LANES).astype(BF16),
                jnp.pad(a_w_gk2[l], ((0, LANES - GLA_GATE_RANK), (0, 0))), a_b_gk[l].reshape(1, -1))
            o = _gla_core(qkvr, logg, a_norm_w[l].reshape(1, -1), b, s)
            w_out = a_w_out[l]
        else:
            j = l - N_A_LAYERS
            lambda_init = 0.8 - 0.6 * math.exp(-0.3 * l)
            w_qkv = jnp.concatenate([b_w_q[j], kv_w], axis=1).astype(BF16)
            qkv = _qkv_rope(xc, w_qkv, pos_col, inv_row)
            o = _diff_attn(qkv, b_lambda[j], b_subln_w[j].reshape(1, -1), b, s, lambda_init)
            w_out = b_w_out[j]
        rw = _pad_cols(router_w[l], LANES)
        rw_hi = rw.astype(BF16)
        rw_lo = (rw - rw_hi.astype(F32)).astype(BF16)
        x1, logits = _proj_ln(o, w_out.astype(BF16), xc, ln1_g[l].reshape(1, -1), ln1_b[l].reshape(1, -1),
                              jnp.concatenate([rw_hi, rw_lo], axis=1), _pad_cols(router_b[l].reshape(1, -1), LANES))
        xc = _moe(x1, logits, ln2_g[l].reshape(1, -1), ln2_b[l].reshape(1, -1),
                  moe_w_gu, moe_b_gu, moe_w_down, moe_b_down, l)
    return xc.reshape(b, s, d)
```

```python
import functools
import math

import jax
import jax.numpy as jnp
from jax import lax
from jax.experimental import pallas as pl
from jax.experimental.pallas import tpu as pltpu

F32 = jnp.float32
BF16 = jnp.bfloat16
HIGHEST = lax.Precision.HIGHEST

DEPTH = 2
N_A_LAYERS = DEPTH // 2
GLA_HEADS = 4
GLA_DK = 128
GLA_DV = 256
GLA_GATE_RANK = 16
GLA_GATE_TAU = 16.0
GLA_CHUNK = 64
DIFF_HEADS = 8
DIFF_HD = 64
DIFF_VD = 128
ROPE_THETA = 10000.0
N_EXPERTS = 32
TOP_K = 4
SWIGLU_LIMIT = 7.0
SWIGLU_ALPHA = 1.702
LN_EPS = 1e-5
RMS_EPS = 1e-5
DEEPNORM_ALPHA = (2.0 * DEPTH) ** 0.25

LANES = 128
SUBLANES = 8
MXU_COLS = 256
BF16_EXACT_INT = 256
STRIP_ALIGN = SUBLANES
VMEM_LIMIT_BYTES = 56 * 1024 * 1024

ROW_TILE = 512
EXPERT_ROWS = 512
EXPERT_FC = 512
MOE_TOKENS = 512
ATTN_Q = 512
ATTN_K = 512
GLA_ROWS = 1024


def _params(*sem):
    return pltpu.CompilerParams(dimension_semantics=sem, vmem_limit_bytes=VMEM_LIMIT_BYTES)


def _nt_dot(a, b):
    return lax.dot_general(a, b, (((1,), (1,)), ((), ())), preferred_element_type=F32)


def _gla_proj_kernel(x_ref, w_ref, wg_ref, wgk2_ref, bgk_ref, qkvr_ref, logg_ref):
    xb = x_ref[...].astype(BF16)
    n = w_ref.shape[1]
    for j in range(0, n, 512):
        qkvr_ref[:, j:j + 512] = jnp.dot(xb, w_ref[:, j:j + 512],
                                         preferred_element_type=F32).astype(BF16)
    gk_low = jnp.dot(xb, wg_ref[...], preferred_element_type=F32)
    z = jnp.dot(gk_low, wgk2_ref[...], preferred_element_type=F32, precision=HIGHEST) + bgk_ref[...]
    logg_ref[...] = (jnp.minimum(z, 0.0) - jnp.log1p(jnp.exp(-jnp.abs(z)))) / GLA_GATE_TAU


def _gla_proj(x2d, w_main, w_gate, w_gk2, b_gk):
    t, d = x2d.shape
    n = w_main.shape[1]
    hk = b_gk.shape[1]
    tm = min(ROW_TILE, t)
    return pl.pallas_call(
        _gla_proj_kernel,
        grid=(t // tm,),
        in_specs=[
            pl.BlockSpec((tm, d), lambda i: (i, 0)),
            pl.BlockSpec((d, n), lambda i: (0, 0)),
            pl.BlockSpec((d, LANES), lambda i: (0, 0)),
            pl.BlockSpec((LANES, hk), lambda i: (0, 0)),
            pl.BlockSpec((1, hk), lambda i: (0, 0)),
        ],
        out_specs=[
            pl.BlockSpec((tm, n), lambda i: (i, 0)),
            pl.BlockSpec((tm, hk), lambda i: (i, 0)),
        ],
        out_shape=[jax.ShapeDtypeStruct((t, n), BF16), jax.ShapeDtypeStruct((t, hk), F32)],
        compiler_params=_params("arbitrary"),
        name="gla_proj",
    )(x2d, w_main, w_gate, w_gk2, b_gk)


def _gla_kernel(q_ref, k_ref, v_ref, r_ref, lg_ref, nw_ref, o_ref, st_ref):
    c = GLA_CHUNK

    @pl.when(pl.program_id(2) == 0)
    def _():
        st_ref[...] = jnp.zeros_like(st_ref)

    row = lax.broadcasted_iota(jnp.int32, (c, c), 0)
    col = lax.broadcasted_iota(jnp.int32, (c, c), 1)
    tril = row >= col
    row_id = lax.broadcasted_iota(jnp.int32, (c, GLA_DK), 0)
    n_chunks = q_ref.shape[0] // c

    st = st_ref[...]
    for ci in range(n_chunks):
        r0 = ci * c
        lg = lg_ref[pl.ds(r0, c), :]
        g = lg
        shift = 1
        while shift < c:
            g = g + jnp.where(row_id >= shift, pltpu.roll(g, shift, 0), 0.0)
            shift *= 2
        g_last = g[c - 1:c, :]
        q = q_ref[pl.ds(r0, c), :].astype(F32) * (GLA_DK ** -0.5)
        k = k_ref[pl.ds(r0, c), :].astype(F32)
        v = v_ref[pl.ds(r0, c), :]
        q_in = (q * jnp.exp(g)).astype(BF16)
        k_in = (k * jnp.exp(-g)).astype(BF16)
        k_state = (k * jnp.exp(g_last - g)).astype(BF16)
        decay = jnp.exp(g_last)
        scores = jnp.where(tril, _nt_dot(q_in, k_in), 0.0).astype(BF16)
        o = jnp.dot(scores, v, preferred_element_type=F32) + _nt_dot(q_in, st.astype(BF16))
        v_t = v.astype(F32).T.astype(BF16)
        st = st * decay + jnp.dot(v_t, k_state, preferred_element_type=F32)
        o = o * lax.rsqrt(jnp.mean(o * o, axis=-1, keepdims=True) + RMS_EPS) * nw_ref[...]
        r = r_ref[pl.ds(r0, c), :].astype(F32)
        o_ref[pl.ds(r0, c), :] = (o * (r * jax.nn.sigmoid(r))).astype(BF16)
    st_ref[...] = st


def _gla_core(qkvr, logg, norm_w, b, s):
    t = b * s
    h, dk, dv = GLA_HEADS, GLA_DK, GLA_DV
    ts = min(GLA_ROWS, s)
    ns = s // ts
    k_off = (h * dk) // dk
    v_off = (2 * h * dk) // dv
    r_off = (2 * h * dk + h * dv) // dv
    return pl.pallas_call(
        _gla_kernel,
        grid=(b, h, ns),
        in_specs=[
            pl.BlockSpec((ts, dk), lambda bi, hi, si: (bi * ns + si, hi)),
            pl.BlockSpec((ts, dk), lambda bi, hi, si: (bi * ns + si, k_off + hi)),
            pl.BlockSpec((ts, dv), lambda bi, hi, si: (bi * ns + si, v_off + hi)),
            pl.BlockSpec((ts, dv), lambda bi, hi, si: (bi * ns + si, r_off + hi)),
            pl.BlockSpec((ts, dk), lambda bi, hi, si: (bi * ns + si, hi)),
            pl.BlockSpec((1, dv), lambda bi, hi, si: (0, 0)),
        ],
        out_specs=pl.BlockSpec((ts, dv), lambda bi, hi, si: (bi * ns + si, hi)),
        out_shape=jax.ShapeDtypeStruct((t, h * dv), BF16),
        scratch_shapes=[pltpu.VMEM((dv, dk), F32)],
        compiler_params=_params("arbitrary", "arbitrary", "arbitrary"),
        name="gla_core",
    )(qkvr, qkvr, qkvr, qkvr, logg, norm_w)


def _layer_norm(y, g, b):
    mu = jnp.mean(y, axis=-1, keepdims=True)
    yc = y - mu
    var = jnp.mean(yc * yc, axis=-1, keepdims=True)
    return yc * lax.rsqrt(var + LN_EPS) * g + b


def _proj_ln_kernel(o_ref, w_ref, x_ref, g_ref, b_ref, rw_ref, rb_ref, x1_ref, logit_ref):
    h = jnp.dot(o_ref[...], w_ref[...], preferred_element_type=F32)
    x1 = _layer_norm(DEEPNORM_ALPHA * x_ref[...] + h, g_ref[...], b_ref[...])
    x1_ref[...] = x1
    xh = x1.astype(BF16)
    xl = (x1 - xh.astype(F32)).astype(BF16)
    hw = jnp.dot(xh, rw_ref[...], preferred_element_type=F32)
    lw = jnp.dot(xl, rw_ref[:, 0:LANES], preferred_element_type=F32)
    logit_ref[...] = hw[:, 0:LANES] + hw[:, LANES:2 * LANES] + lw + rb_ref[...]


def _proj_ln(o, w_out, x2d, ln_g, ln_b, rw, rb):
    t, d = x2d.shape
    kdim = o.shape[1]
    tm = min(ROW_TILE, t)
    return pl.pallas_call(
        _proj_ln_kernel,
        grid=(t // tm,),
        in_specs=[
            pl.BlockSpec((tm, kdim), lambda i: (i, 0)),
            pl.BlockSpec((kdim, d), lambda i: (0, 0)),
            pl.BlockSpec((tm, d), lambda i: (i, 0)),
            pl.BlockSpec((1, d), lambda i: (0, 0)),
            pl.BlockSpec((1, d), lambda i: (0, 0)),
            pl.BlockSpec((d, 2 * LANES), lambda i: (0, 0)),
            pl.BlockSpec((1, LANES), lambda i: (0, 0)),
        ],
        out_specs=[
            pl.BlockSpec((tm, d), lambda i: (i, 0)),
            pl.BlockSpec((tm, LANES), lambda i: (i, 0)),
        ],
        out_shape=[jax.ShapeDtypeStruct((t, d), F32), jax.ShapeDtypeStruct((t, LANES), F32)],
        compiler_params=_params("arbitrary"),
        name="proj_ln",
    )(o, w_out, x2d, ln_g, ln_b, rw, rb)


def _route_kernel(lt_ref, lpos_ref, gate_ref, cnt_ref, before_ref, carry_ref):
    e, tb = lt_ref.shape

    @pl.when(pl.program_id(0) == 0)
    def _():
        carry_ref[...] = jnp.zeros_like(carry_ref)

    vals = lt_ref[...]
    eidx = lax.broadcasted_iota(jnp.int32, (e, tb), 0).astype(F32)
    top_v, hots = [], []
    for _ in range(TOP_K):
        m = jnp.max(vals, axis=0, keepdims=True)
        idx = jnp.min(jnp.where(vals == m, eidx, float(e)), axis=0, keepdims=True)
        hot = eidx == idx
        vals = jnp.where(hot, -jnp.inf, vals)
        top_v.append(m)
        hots.append(hot)
    ex = [jnp.exp(v - top_v[0]) for v in top_v]
    den = ex[0] + ex[1] + ex[2] + ex[3]
    gates = [x / den for x in ex]
    multi = jnp.zeros((e, tb), F32)
    for hot in hots:
        multi = multi + jnp.where(hot, 1.0, 0.0)
    ri = lax.broadcasted_iota(jnp.int32, (tb, tb), 0)
    ci = lax.broadcasted_iota(jnp.int32, (tb, tb), 1)
    upper = jnp.where(ri <= ci, 1.0, 0.0).astype(BF16)
    incl = jnp.dot(multi.astype(BF16), upper, preferred_element_type=F32)
    cnt = jnp.broadcast_to(incl[:, tb - 1:tb], (e, LANES))
    cnt = jnp.ceil(cnt * (1.0 / STRIP_ALIGN)) * STRIP_ALIGN
    er = lax.broadcasted_iota(jnp.int32, (e, e), 0)
    ec = lax.broadcasted_iota(jnp.int32, (e, e), 1)
    lower = jnp.where(er > ec, 1.0, 0.0).astype(BF16)
    cnt_hi = jnp.floor(cnt * (1.0 / BF16_EXACT_INT)) * BF16_EXACT_INT
    cnt_lo = cnt - cnt_hi
    start = (jnp.dot(lower, cnt_hi.astype(BF16), preferred_element_type=F32)
             + jnp.dot(lower, cnt_lo.astype(BF16), preferred_element_type=F32))[:, 0:1]
    where_to = start + incl - multi
    lpos = [jnp.sum(jnp.where(hot, where_to, 0.0), axis=0, keepdims=True) for hot in hots]
    lpos_ref[...] = jnp.concatenate(lpos + lpos, axis=0).astype(jnp.int32)
    gate_ref[...] = jnp.concatenate(gates + [jnp.zeros_like(g) for g in gates], axis=0)
    cnt_ref[...] = cnt
    before_ref[...] = carry_ref[...]
    carry_ref[...] = carry_ref[...] + cnt


def _route(logits_t):
    e, t = logits_t.shape
    tb = min(MOE_TOKENS, t)
    nblk = t // tb
    return pl.pallas_call(
        _route_kernel,
        grid=(nblk,),
        in_specs=[pl.BlockSpec((e, tb), lambda i: (0, i))],
        out_specs=[
            pl.BlockSpec((2 * TOP_K, tb), lambda i: (0, i)),
            pl.BlockSpec((2 * TOP_K, tb), lambda i: (0, i)),
            pl.BlockSpec((None, e, LANES), lambda i: (i, 0, 0)),
            pl.BlockSpec((None, e, LANES), lambda i: (i, 0, 0)),
        ],
        out_shape=[
            jax.ShapeDtypeStruct((2 * TOP_K, t), jnp.int32),
            jax.ShapeDtypeStruct((2 * TOP_K, t), F32),
            jax.ShapeDtypeStruct((nblk, e, LANES), F32),
            jax.ShapeDtypeStruct((nblk, e, LANES), F32),
        ],
        scratch_shapes=[pltpu.VMEM((e, LANES), F32)],
        compiler_params=_params("arbitrary"),
        name="route",
    )(logits_t)


def _local_rows(tb):
    return TOP_K * tb + STRIP_ALIGN * N_EXPERTS


def _for_each_strip(cnt_ref, dst_ref, blk, fn):
    def per_expert(e, local_row):
        c = cnt_ref[blk * N_EXPERTS + e]
        g = dst_ref[blk * N_EXPERTS + e]

        @pl.when(c > 0)
        def _():
            fn(pl.multiple_of(local_row, STRIP_ALIGN), pl.multiple_of(g, STRIP_ALIGN), pl.multiple_of(c, STRIP_ALIGN))

        return local_row + c

    lax.fori_loop(0, N_EXPERTS, per_expert, jnp.int32(0))


def _dispatch_kernel(cnt_ref, dst_ref, tail_ref, x_ref, lpos_ref, xb_hbm, ls_ref, zero_ref, sem, zero_sem):
    i = pl.program_id(0)
    tb = x_ref.shape[0]
    n = ls_ref.shape[1]
    bm = zero_ref.shape[0]

    @pl.when(i == 0)
    def _():
        zero_ref[...] = jnp.zeros_like(zero_ref)
        n_blocks = xb_hbm.shape[0] // bm
        n_used = tail_ref[N_EXPERTS]

        def zero_copy(row):
            return pltpu.make_async_copy(zero_ref, xb_hbm.at[pl.ds(pl.multiple_of(row, STRIP_ALIGN), bm)], zero_sem)

        def start(e, c):
            @pl.when(tail_ref[e] >= 0)
            def _():
                zero_copy(tail_ref[e]).start()
            return c

        def wait(e, c):
            @pl.when(tail_ref[e] >= 0)
            def _():
                zero_copy(tail_ref[e]).wait()
            return c

        def start_unused(blk, c):
            zero_copy(blk * bm).start()
            return c

        def wait_unused(blk, c):
            zero_copy(blk * bm).wait()
            return c

        lax.fori_loop(0, N_EXPERTS, start, 0)
        lax.fori_loop(n_used, n_blocks, start_unused, 0)
        lax.fori_loop(0, N_EXPERTS, wait, 0)
        lax.fori_loop(n_used, n_blocks, wait_unused, 0)

    r = lax.broadcasted_iota(jnp.int32, (n, tb), 0)
    lp = lpos_ref[...]
    perm = jnp.zeros((n, tb), F32)
    for k in range(TOP_K):
        perm = jnp.where(r == lp[k:k + 1], 1.0, perm)
    slot = i % 2
    ls_ref[slot] = jnp.dot(perm.astype(BF16), x_ref[...].astype(BF16), preferred_element_type=F32)

    def strips(blk, blk_slot, act):
        def piece(local_row, global_row, rows):
            cp = pltpu.make_async_copy(ls_ref.at[blk_slot, pl.ds(local_row, rows)],
                                       xb_hbm.at[pl.ds(global_row, rows)], sem.at[blk_slot])
            cp.start() if act == "start" else cp.wait()
        _for_each_strip(cnt_ref, dst_ref, blk, piece)

    strips(i, slot, "start")

    @pl.when(i > 0)
    def _():
        strips(i - 1, 1 - slot, "wait")

    @pl.when(i == pl.num_programs(0) - 1)
    def _():
        strips(i, slot, "wait")


def _dispatch(x1, lpos, cnt_flat, dst_flat, tails, n_pad):
    t, d = x1.shape
    tb = min(MOE_TOKENS, t)
    return pl.pallas_call(
        _dispatch_kernel,
        grid_spec=pltpu.PrefetchScalarGridSpec(
            num_scalar_prefetch=3,
            grid=(t // tb,),
            in_specs=[
                pl.BlockSpec((tb, d), lambda i, c, g, tl: (i, 0)),
                pl.BlockSpec((2 * TOP_K, tb), lambda i, c, g, tl: (0, i)),
            ],
            out_specs=pl.BlockSpec(memory_space=pl.ANY),
            scratch_shapes=[
                pltpu.VMEM((2, _local_rows(tb), d), F32),
                pltpu.VMEM((EXPERT_ROWS, d), F32),
                pltpu.SemaphoreType.DMA((2,)),
                pltpu.SemaphoreType.DMA,
            ],
        ),
        out_shape=jax.ShapeDtypeStruct((n_pad, d), F32),
        compiler_params=_params("arbitrary"),
        name="moe_dispatch",
    )(cnt_flat, dst_flat, tails, x1, lpos)


def _expert_kernel(be_ref, nu_ref, x_ref, wgu_ref, bgu_ref, wd_ref, bd_ref, y_ref, wgu_b, wd_b):
    i = pl.program_id(0)
    f = wd_ref.shape[0]

    @pl.when(i >= nu_ref[0])
    def _():
        y_ref[...] = jnp.zeros_like(y_ref)

    @pl.when(i < nu_ref[0])
    def _():
        prev = be_ref[jnp.maximum(i - 1, 0)]

        @pl.when((i == 0) | (be_ref[i] != prev))
        def _():
            wgu_b[...] = wgu_ref[...].astype(BF16)
            wd_b[...] = wd_ref[...].astype(BF16)

        xb = x_ref[...].astype(BF16)
        fc = EXPERT_FC
        acc = jnp.zeros(y_ref.shape, F32) + bd_ref[...]
        for j in range(0, f, fc):
            h_glu = jnp.dot(xb, wgu_b[:, j:j + fc], preferred_element_type=F32) + bgu_ref[:, j:j + fc]
            h_lin = (jnp.dot(xb, wgu_b[:, f + j:f + j + fc], preferred_element_type=F32)
                     + bgu_ref[:, f + j:f + j + fc])
            x_glu = jnp.minimum(h_glu, SWIGLU_LIMIT)
            x_lin = jnp.clip(h_lin, -SWIGLU_LIMIT, SWIGLU_LIMIT)
            act = x_glu * jax.nn.sigmoid(SWIGLU_ALPHA * x_glu) * (x_lin + 1.0)
            acc = acc + jnp.dot(act.astype(BF16), wd_b[j:j + fc, :], preferred_element_type=F32)
        y_ref[...] = acc


def _expert_ffn(xb, blk_e, n_used, w_gu, b_gu, w_down, b_down, layer):
    n_pad, d = xb.shape
    _, e, _, f2 = w_gu.shape
    f = f2 // 2
    bm = EXPERT_ROWS
    nb = n_pad // bm
    b_gu4 = b_gu.reshape(b_gu.shape[0], e, 1, f2)
    b_down4 = b_down.reshape(b_down.shape[0], e, 1, d)

    def row_map(i, be, nu):
        return (jnp.minimum(i, nu[0] - 1), 0)

    def w_map(i, be, nu):
        return (layer, be[i], 0, 0)

    return pl.pallas_call(
        _expert_kernel,
        grid_spec=pltpu.PrefetchScalarGridSpec(
            num_scalar_prefetch=2,
            grid=(nb,),
            in_specs=[
                pl.BlockSpec((bm, d), row_map),
                pl.BlockSpec((None, None, d, f2), w_map),
                pl.BlockSpec((None, None, 1, f2), w_map),
                pl.BlockSpec((None, None, f, d), w_map),
                pl.BlockSpec((None, None, 1, d), w_map),
            ],
            out_specs=pl.BlockSpec((bm, d), lambda i, be, nu: (i, 0)),
            scratch_shapes=[pltpu.VMEM((d, f2), BF16), pltpu.VMEM((f, d), BF16)],
        ),
        out_shape=jax.ShapeDtypeStruct((n_pad, d), F32),
        compiler_params=_params("arbitrary"),
        name="moe_experts",
    )(blk_e, n_used, xb, w_gu, b_gu4, w_down, b_down4)


def _combine_kernel(cnt_ref, dst_ref, x_ref, lpos_ref, gate_ref, g_ref, b_ref, yb_hbm, o_ref, ly_ref, sem):
    i = pl.program_id(0)
    tb = x_ref.shape[0]
    n = ly_ref.shape[1]
    slot = i % 2

    def strips(blk, blk_slot, act):
        def piece(local_row, global_row, rows):
            cp = pltpu.make_async_copy(yb_hbm.at[pl.ds(global_row, rows)],
                                       ly_ref.at[blk_slot, pl.ds(local_row, rows)], sem.at[blk_slot])
            cp.start() if act == "start" else cp.wait()
        _for_each_strip(cnt_ref, dst_ref, blk, piece)

    @pl.when(i == 0)
    def _():
        ly_ref[...] = jnp.zeros_like(ly_ref)
        strips(0, 0, "start")

    @pl.when(i + 1 < pl.num_programs(0))
    def _():
        strips(i + 1, 1 - slot, "start")

    strips(i, slot, "wait")

    c = lax.broadcasted_iota(jnp.int32, (tb, n), 1)
    w = jnp.zeros((tb, n), F32)
    for k in range(TOP_K):
        w = jnp.where(c == lpos_ref[:, k:k + 1], gate_ref[:, k:k + 1], w)
    m = jnp.dot(w.astype(BF16), ly_ref[slot].astype(BF16), preferred_element_type=F32)
    o_ref[...] = _layer_norm(DEEPNORM_ALPHA * x_ref[...] + m, g_ref[...], b_ref[...])


def _combine(x1, lpos_t, gates_t, ln_g, ln_b, cnt_flat, dst_flat, yb):
    t, d = x1.shape
    tb = min(MOE_TOKENS, t)
    return pl.pallas_call(
        _combine_kernel,
        grid_spec=pltpu.PrefetchScalarGridSpec(
            num_scalar_prefetch=2,
            grid=(t // tb,),
            in_specs=[
                pl.BlockSpec((tb, d), lambda i, c, g: (i, 0)),
                pl.BlockSpec((tb, 2 * TOP_K), lambda i, c, g: (i, 0)),
                pl.BlockSpec((tb, 2 * TOP_K), lambda i, c, g: (i, 0)),
                pl.BlockSpec((1, d), lambda i, c, g: (0, 0)),
                pl.BlockSpec((1, d), lambda i, c, g: (0, 0)),
                pl.BlockSpec(memory_space=pl.ANY),
            ],
            out_specs=pl.BlockSpec((tb, d), lambda i, c, g: (i, 0)),
            scratch_shapes=[
                pltpu.VMEM((2, _local_rows(tb), d), F32),
                pltpu.SemaphoreType.DMA((2,)),
            ],
        ),
        out_shape=jax.ShapeDtypeStruct((t, d), F32),
        compiler_params=_params("arbitrary"),
        name="moe_combine",
    )(cnt_flat, dst_flat, x1, lpos_t, gates_t, ln_g, ln_b, yb)


def _moe(x1, logits, ln_g, ln_b, w_gu, b_gu, w_down, b_down, layer):
    t, d = x1.shape
    e = N_EXPERTS
    bm = EXPERT_ROWS
    lpos, gates, cnt_blk, before_blk = _route(logits[:, :e].T)
    cnt_blk = cnt_blk[:, :, 0].astype(jnp.int32)
    before_blk = before_blk[:, :, 0].astype(jnp.int32)
    cnt = before_blk[-1] + cnt_blk[-1]
    padded = (cnt + bm - 1) // bm * bm
    pad_end = jnp.cumsum(padded)
    pad_start = pad_end - padded
    dst_blk = pad_start[None, :] + before_blk
    nb = -(-(t * TOP_K + (STRIP_ALIGN - 1) * cnt_blk.shape[0] * e) // bm) + e
    n_used = (pad_end[-1] // bm).astype(jnp.int32)
    blk_start = jnp.arange(nb, dtype=jnp.int32) * bm
    blk_e = jnp.minimum(jnp.sum(pad_end[None, :] <= blk_start[:, None], axis=1), e - 1).astype(jnp.int32)
    blk_e = jnp.where(jnp.arange(nb) < n_used, blk_e, blk_e[jnp.maximum(n_used - 1, 0)])
    tails = jnp.concatenate([jnp.where(padded > 0, pad_end - bm, -1), n_used.reshape(1)]).astype(jnp.int32)
    cnt_flat = cnt_blk.reshape(-1)
    dst_flat = dst_blk.reshape(-1).astype(jnp.int32)
    xb = _dispatch(x1, lpos, cnt_flat, dst_flat, tails, nb * bm)
    yb = _expert_ffn(xb, blk_e, n_used.reshape(1), w_gu, b_gu, w_down, b_down, layer)
    return _combine(x1, lpos.T, gates.T, ln_g, ln_b, cnt_flat, dst_flat, yb)


def _qkv_rope_kernel(x_ref, w_ref, pos_ref, inv_ref, out_ref):
    xb = x_ref[...].astype(BF16)
    tm = x_ref.shape[0]
    n_rope = 2 * (2 * DIFF_HEADS * DIFF_HD)
    n_freq = DIFF_HD // 2
    groups = LANES // n_freq
    rows = tm // groups
    pos = pos_ref[...].astype(F32)
    lane_c = lax.broadcasted_iota(jnp.int32, (rows, LANES), 1)
    pos_c = pos[(groups - 1) * rows:groups * rows]
    for g in reversed(range(groups - 1)):
        pos_c = jnp.where(lane_c < (g + 1) * n_freq, pos[g * rows:(g + 1) * rows], pos_c)
    ang = pos_c * inv_ref[...]

    def spread(tbl):
        out = []
        for g in range(groups):
            y = pltpu.roll(tbl, LANES - g * n_freq, 1) if g else tbl
            width = n_freq
            while width < LANES:
                y = jnp.where(lane_c < width, y, pltpu.roll(y, width, 1))
                width *= 2
            out.append(y)
        return jnp.concatenate(out, axis=0)

    cos = spread(jnp.cos(ang))
    sin = spread(jnp.sin(ang))
    lane = lax.broadcasted_iota(jnp.int32, (tm, LANES), 1)
    first_half = (lane % DIFF_HD) < (DIFF_HD // 2)
    sin_signed = jnp.where(first_half, -sin, sin)
    q_scale = DIFF_HD ** -0.5 * math.log2(math.e)
    n = w_ref.shape[1]
    for j0 in range(0, n, MXU_COLS):
        wide = jnp.dot(xb, w_ref[:, j0:j0 + MXU_COLS], preferred_element_type=F32)
        for j in range(j0, j0 + MXU_COLS, LANES):
            c = wide[:, j - j0:j - j0 + LANES]
            if j < n_rope:
                rot = jnp.where(first_half, pltpu.roll(c, LANES - DIFF_HD // 2, 1), pltpu.roll(c, DIFF_HD // 2, 1))
                c = c * cos + rot * sin_signed
                if j < n_rope // 2:
                    c = c * q_scale
            out_ref[:, j:j + LANES] = c.astype(BF16)


def _qkv_rope(x2d, w_qkv, pos_col, inv_row):
    t, d = x2d.shape
    n = w_qkv.shape[1]
    tm = min(ROW_TILE, t)
    return pl.pallas_call(
        _qkv_rope_kernel,
        grid=(t // tm,),
        in_specs=[
            pl.BlockSpec((tm, d), lambda i: (i, 0)),
            pl.BlockSpec((d, n), lambda i: (0, 0)),
            pl.BlockSpec((tm, 1), lambda i: (i, 0)),
            pl.BlockSpec((1, LANES), lambda i: (0, 0)),
        ],
        out_specs=pl.BlockSpec((tm, n), lambda i: (i, 0)),
        out_shape=jax.ShapeDtypeStruct((t, n), BF16),
        compiler_params=_params("arbitrary"),
        name="qkv_rope",
    )(x2d, w_qkv, pos_col, inv_row)


def _diff_attn_kernel(q_ref, k_ref, v_ref, lam_ref, sw_ref, o_ref, qs_ref, m_ref, l_ref, acc_ref, bias_ref, *,
                      lambda_init):
    qi = pl.program_id(2)
    tq = q_ref.shape[0]
    tk = bias_ref.shape[2]

    @pl.when((pl.program_id(0) == 0) & (pl.program_id(1) == 0) & (qi == 0))
    def _():
        for d in range(tq // tk):
            qrow = lax.broadcasted_iota(jnp.int32, (2 * tq, tk), 0) % tq
            kcol = d * tk + lax.broadcasted_iota(jnp.int32, (2 * tq, tk), 1)
            bias_ref[d] = jnp.where(kcol <= qrow, 0.0, -jnp.inf)

    q = q_ref[...]
    lane = lax.broadcasted_iota(jnp.int32, q.shape, 1)
    zero = jnp.zeros_like(q)
    qs_ref[0:tq, :] = jnp.where(lane < DIFF_HD, q, zero)
    qs_ref[tq:2 * tq, :] = jnp.where(lane >= DIFF_HD, q, zero)
    m_ref[...] = jnp.full(m_ref.shape, -jnp.inf, F32)
    l_ref[...] = jnp.zeros_like(l_ref)
    acc_ref[...] = jnp.zeros_like(acc_ref)

    def block(kj, diag=None):
        k0 = pl.multiple_of(kj * tk, tk)
        s = _nt_dot(qs_ref[...], k_ref[pl.ds(k0, tk), :])
        if diag is not None:
            s = s + bias_ref[diag]
        m_old = m_ref[...]
        m_new = jnp.maximum(m_old, jnp.max(s, axis=-1, keepdims=True))
        p = jnp.exp2((s - jnp.concatenate([m_new] * (tk // LANES), axis=1)).astype(BF16))
        alpha = jnp.exp2(m_old - m_new)
        l_ref[...] = alpha * l_ref[...] + jnp.sum(p.astype(F32), axis=-1, keepdims=True)
        acc_ref[...] = alpha * acc_ref[...] + jnp.dot(p, v_ref[pl.ds(k0, tk), :], preferred_element_type=F32)
        m_ref[...] = m_new

    n_full = (qi * tq) // tk

    def pair(kp, c):
        block(2 * kp)
        block(2 * kp + 1)
        return c

    def single(kj, c):
        block(kj)
        return c

    lax.fori_loop(0, n_full // 2, pair, 0)
    lax.fori_loop((n_full // 2) * 2, n_full, single, 0)
    for d in range(tq // tk):
        block(n_full + d, d)

    lam = lam_ref[...]
    lam_full = (jnp.exp(jnp.sum(lam[0:1] * lam[1:2], axis=-1, keepdims=True))
                - jnp.exp(jnp.sum(lam[2:3] * lam[3:4], axis=-1, keepdims=True)) + lambda_init)
    o1 = acc_ref[0:tq, :] / l_ref[0:tq, :]
    o2 = acc_ref[tq:2 * tq, :] / l_ref[tq:2 * tq, :]
    o = o1 - lam_full * o2
    o = o * lax.rsqrt(jnp.mean(o * o, axis=-1, keepdims=True) + RMS_EPS) * sw_ref[...]
    o_ref[...] = (o * (1.0 - lambda_init)).astype(BF16)


def _diff_attn(qkv, lam, subln_w, b, s, lambda_init):
    t = b * s
    h, w = DIFF_HEADS, 2 * DIFF_HD
    tq = min(ATTN_Q, s)
    tk = min(ATTN_K, tq)
    assert tq % tk == 0
    nq = s // tq
    return pl.pallas_call(
        functools.partial(_diff_attn_kernel, lambda_init=lambda_init),
        grid=(b, h, nq),
        in_specs=[
            pl.BlockSpec((tq, w), lambda bi, hi, qi: (bi * nq + qi, hi)),
            pl.BlockSpec((s, w), lambda bi, hi, qi: (bi, h + hi)),
            pl.BlockSpec((s, DIFF_VD), lambda bi, hi, qi: (bi, 2 * h + hi)),
            pl.BlockSpec((4, DIFF_HD), lambda bi, hi, qi: (0, 0)),
            pl.BlockSpec((1, DIFF_VD), lambda bi, hi, qi: (0, 0)),
        ],
        out_specs=pl.BlockSpec((tq, DIFF_VD), lambda bi, hi, qi: (bi * nq + qi, hi)),
        out_shape=jax.ShapeDtypeStruct((t, h * DIFF_VD), BF16),
        scratch_shapes=[
            pltpu.VMEM((2 * tq, w), BF16),
            pltpu.VMEM((2 * tq, LANES), F32),
            pltpu.VMEM((2 * tq, LANES), F32),
            pltpu.VMEM((2 * tq, DIFF_VD), F32),
            pltpu.VMEM((tq // tk, 2 * tq, tk), F32),
        ],
        compiler_params=_params("arbitrary", "arbitrary", "arbitrary"),
        name="diff_attn",
    )(qkv, qkv, qkv, lam, subln_w)


def _pad_cols(a, n):
    return jnp.pad(a, ((0, 0), (0, n - a.shape[1])))


def kernel(x, positions, a_w_in, a_w_gk2, a_b_gk, a_norm_w, a_w_out, kv_w, b_w_q, b_lambda, b_subln_w,
           b_w_out, ln1_g, ln1_b, ln2_g, ln2_b, router_w, router_b, moe_w_gu, moe_b_gu, moe_w_down,
           moe_b_down):
    b, s, d = x.shape
    t = b * s
    xc = x.reshape(t, d)
    n_main = 2 * GLA_HEADS * GLA_DK + 2 * GLA_HEADS * GLA_DV
    half = DIFF_HD // 2
    inv = ROPE_THETA ** (-jnp.arange(half, dtype=F32) * 2.0 / DIFF_HD)
    inv_row = jnp.tile(inv, LANES // half).reshape(1, LANES)
    pos_col = positions.reshape(t, 1)
    assert DEPTH - N_A_LAYERS == 1
    for l in range(DEPTH):
        if l < N_A_LAYERS:
            w_in = a_w_in[l]
            qkvr, logg = _gla_proj(
                xc, w_in[:, :n_main].astype(BF16), _pad_cols(w_in[:, n_main:], LANES).astype(BF16),
                jnp.pad(a_w_gk2[l], ((0, LANES - GLA_GATE_RANK), (0, 0))), a_b_gk[l].reshape(1, -1))
            o = _gla_core(qkvr, logg, a_norm_w[l].reshape(1, -1), b, s)
            w_out = a_w_out[l]
        else:
            j = l - N_A_LAYERS
            lambda_init = 0.8 - 0.6 * math.exp(-0.3 * l)
            w_qkv = jnp.concatenate([b_w_q[j], kv_w], axis=1).astype(BF16)
            qkv = _qkv_rope(xc, w_qkv, pos_col, inv_row)
            o = _diff_attn(qkv, b_lambda[j], b_subln_w[j].reshape(1, -1), b, s, lambda_init)
            w_out = b_w_out[j]
        rw = _pad_cols(router_w[l], LANES)
        rw_hi = rw.astype(BF16)
        rw_lo = (rw - rw_hi.astype(F32)).astype(BF16)
        x1, logits = _proj_ln(o, w_out.astype(BF16), xc, ln1_g[l].reshape(1, -1), ln1_b[l].reshape(1, -1),
                              jnp.concatenate([rw_hi, rw_lo], axis=1), _pad_cols(router_b[l].reshape(1, -1), LANES))
        xc = _moe(x1, logits, ln2_g[l].reshape(1, -1), ln2_b[l].reshape(1, -1),
                  moe_w_gu, moe_b_gu, moe_w_down, moe_b_down, l)
    return xc.reshape(b, s, d)
```

```python
import functools
import math

import jax
import jax.numpy as jnp
from jax import lax
from jax.experimental import pallas as pl
from jax.experimental.pallas import tpu as pltpu

F32 = jnp.float32
BF16 = jnp.bfloat16
HIGHEST = lax.Precision.HIGHEST

DEPTH = 2
N_A_LAYERS = DEPTH // 2
GLA_HEADS = 4
GLA_DK = 128
GLA_DV = 256
GLA_GATE_RANK = 16
GLA_GATE_TAU = 16.0
GLA_CHUNK = 64
DIFF_HEADS = 8
DIFF_HD = 64
DIFF_VD = 128
ROPE_THETA = 10000.0
N_EXPERTS = 32
TOP_K = 4
SWIGLU_LIMIT = 7.0
SWIGLU_ALPHA = 1.702
LN_EPS = 1e-5
RMS_EPS = 1e-5
DEEPNORM_ALPHA = (2.0 * DEPTH) ** 0.25

LANES = 128
SUBLANES = 8
MXU_COLS = 256
BF16_EXACT_INT = 256
STRIP_ALIGN = SUBLANES
VMEM_LIMIT_BYTES = 56 * 1024 * 1024

ROW_TILE = 512
EXPERT_ROWS = 512
EXPERT_FC = 512
MOE_TOKENS = 512
MOE_CHUNKS = 3
ATTN_Q = 512
ATTN_K = 512
GLA_ROWS = 1024


def _params(*sem):
    return pltpu.CompilerParams(dimension_semantics=sem, vmem_limit_bytes=VMEM_LIMIT_BYTES)


def _nt_dot(a, b):
    return lax.dot_general(a, b, (((1,), (1,)), ((), ())), preferred_element_type=F32)


def _gla_proj_kernel(x_ref, w_ref, wg_ref, wgk2_ref, bgk_ref, qkvr_ref, logg_ref):
    xb = x_ref[...].astype(BF16)
    n = w_ref.shape[1]
    for j in range(0, n, 512):
        qkvr_ref[:, j:j + 512] = jnp.dot(xb, w_ref[:, j:j + 512],
                                         preferred_element_type=F32).astype(BF16)
    gk_low = jnp.dot(xb, wg_ref[...], preferred_element_type=F32)
    z = jnp.dot(gk_low, wgk2_ref[...], preferred_element_type=F32, precision=HIGHEST) + bgk_ref[...]
    logg_ref[...] = (jnp.minimum(z, 0.0) - jnp.log1p(jnp.exp(-jnp.abs(z)))) / GLA_GATE_TAU


def _gla_proj(x2d, w_main, w_gate, w_gk2, b_gk):
    t, d = x2d.shape
    n = w_main.shape[1]
    hk = b_gk.shape[1]
    tm = min(ROW_TILE, t)
    return pl.pallas_call(
        _gla_proj_kernel,
        grid=(t // tm,),
        in_specs=[
            pl.BlockSpec((tm, d), lambda i: (i, 0)),
            pl.BlockSpec((d, n), lambda i: (0, 0)),
            pl.BlockSpec((d, LANES), lambda i: (0, 0)),
            pl.BlockSpec((LANES, hk), lambda i: (0, 0)),
            pl.BlockSpec((1, hk), lambda i: (0, 0)),
        ],
        out_specs=[
            pl.BlockSpec((tm, n), lambda i: (i, 0)),
            pl.BlockSpec((tm, hk), lambda i: (i, 0)),
        ],
        out_shape=[jax.ShapeDtypeStruct((t, n), BF16), jax.ShapeDtypeStruct((t, hk), F32)],
        compiler_params=_params("arbitrary"),
        name="gla_proj",
    )(x2d, w_main, w_gate, w_gk2, b_gk)


def _gla_kernel(q_ref, k_ref, v_ref, r_ref, lg_ref, nw_ref, o_ref, st_ref):
    c = GLA_CHUNK

    @pl.when(pl.program_id(2) == 0)
    def _():
        st_ref[...] = jnp.zeros_like(st_ref)

    row = lax.broadcasted_iota(jnp.int32, (c, c), 0)
    col = lax.broadcasted_iota(jnp.int32, (c, c), 1)
    tril = row >= col
    row_id = lax.broadcasted_iota(jnp.int32, (c, GLA_DK), 0)
    n_chunks = q_ref.shape[0] // c

    st = st_ref[...]
    for ci in range(n_chunks):
        r0 = ci * c
        lg = lg_ref[pl.ds(r0, c), :]
        g = lg
        shift = 1
        while shift < c:
            g = g + jnp.where(row_id >= shift, pltpu.roll(g, shift, 0), 0.0)
            shift *= 2
        g_last = g[c - 1:c, :]
        q = q_ref[pl.ds(r0, c), :].astype(F32) * (GLA_DK ** -0.5)
        k = k_ref[pl.ds(r0, c), :].astype(F32)
        v = v_ref[pl.ds(r0, c), :]
        q_in = (q * jnp.exp(g)).astype(BF16)
        k_in = (k * jnp.exp(-g)).astype(BF16)
        k_state = (k * jnp.exp(g_last - g)).astype(BF16)
        decay = jnp.exp(g_last)
        scores = jnp.where(tril, _nt_dot(q_in, k_in), 0.0).astype(BF16)
        o = jnp.dot(scores, v, preferred_element_type=F32) + _nt_dot(q_in, st.astype(BF16))
        v_t = v.astype(F32).T.astype(BF16)
        st = st * decay + jnp.dot(v_t, k_state, preferred_element_type=F32)
        o = o * lax.rsqrt(jnp.mean(o * o, axis=-1, keepdims=True) + RMS_EPS) * nw_ref[...]
        r = r_ref[pl.ds(r0, c), :].astype(F32)
        o_ref[pl.ds(r0, c), :] = (o * (r * jax.nn.sigmoid(r))).astype(BF16)
    st_ref[...] = st


def _gla_core(qkvr, logg, norm_w, b, s):
    t = b * s
    h, dk, dv = GLA_HEADS, GLA_DK, GLA_DV
    ts = min(GLA_ROWS, s)
    ns = s // ts
    k_off = (h * dk) // dk
    v_off = (2 * h * dk) // dv
    r_off = (2 * h * dk + h * dv) // dv
    return pl.pallas_call(
        _gla_kernel,
        grid=(b, h, ns),
        in_specs=[
            pl.BlockSpec((ts, dk), lambda bi, hi, si: (bi * ns + si, hi)),
            pl.BlockSpec((ts, dk), lambda bi, hi, si: (bi * ns + si, k_off + hi)),
            pl.BlockSpec((ts, dv), lambda bi, hi, si: (bi * ns + si, v_off + hi)),
            pl.BlockSpec((ts, dv), lambda bi, hi, si: (bi * ns + si, r_off + hi)),
            pl.BlockSpec((ts, dk), lambda bi, hi, si: (bi * ns + si, hi)),
            pl.BlockSpec((1, dv), lambda bi, hi, si: (0, 0)),
        ],
        out_specs=pl.BlockSpec((ts, dv), lambda bi, hi, si: (bi * ns + si, hi)),
        out_shape=jax.ShapeDtypeStruct((t, h * dv), BF16),
        scratch_shapes=[pltpu.VMEM((dv, dk), F32)],
        compiler_params=_params("arbitrary", "arbitrary", "arbitrary"),
        name="gla_core",
    )(qkvr, qkvr, qkvr, qkvr, logg, norm_w)


def _layer_norm(y, g, b):
    mu = jnp.mean(y, axis=-1, keepdims=True)
    yc = y - mu
    var = jnp.mean(yc * yc, axis=-1, keepdims=True)
    return yc * lax.rsqrt(var + LN_EPS) * g + b


def _proj_ln_kernel(o_ref, w_ref, x_ref, g_ref, b_ref, rw_ref, rb_ref, x1_ref, logit_ref):
    h = jnp.dot(o_ref[...], w_ref[...], preferred_element_type=F32)
    x1 = _layer_norm(DEEPNORM_ALPHA * x_ref[...] + h, g_ref[...], b_ref[...])
    x1_ref[...] = x1
    xh = x1.astype(BF16)
    xl = (x1 - xh.astype(F32)).astype(BF16)
    hw = jnp.dot(xh, rw_ref[...], preferred_element_type=F32)
    lw = jnp.dot(xl, rw_ref[:, 0:LANES], preferred_element_type=F32)
    logit_ref[...] = hw[:, 0:LANES] + hw[:, LANES:2 * LANES] + lw + rb_ref[...]


def _proj_ln(o, w_out, x2d, ln_g, ln_b, rw, rb):
    t, d = x2d.shape
    kdim = o.shape[1]
    tm = min(ROW_TILE, t)
    return pl.pallas_call(
        _proj_ln_kernel,
        grid=(t // tm,),
        in_specs=[
            pl.BlockSpec((tm, kdim), lambda i: (i, 0)),
            pl.BlockSpec((kdim, d), lambda i: (0, 0)),
            pl.BlockSpec((tm, d), lambda i: (i, 0)),
            pl.BlockSpec((1, d), lambda i: (0, 0)),
            pl.BlockSpec((1, d), lambda i: (0, 0)),
            pl.BlockSpec((d, 2 * LANES), lambda i: (0, 0)),
            pl.BlockSpec((1, LANES), lambda i: (0, 0)),
        ],
        out_specs=[
            pl.BlockSpec((tm, d), lambda i: (i, 0)),
            pl.BlockSpec((tm, LANES), lambda i: (i, 0)),
        ],
        out_shape=[jax.ShapeDtypeStruct((t, d), F32), jax.ShapeDtypeStruct((t, LANES), F32)],
        compiler_params=_params("arbitrary"),
        name="proj_ln",
    )(o, w_out, x2d, ln_g, ln_b, rw, rb)


def _route_kernel(lt_ref, lpos_ref, gate_ref, cnt_ref, before_ref, carry_ref):
    e, tb = lt_ref.shape

    @pl.when(pl.program_id(0) == 0)
    def _():
        carry_ref[...] = jnp.zeros_like(carry_ref)

    vals = lt_ref[...]
    eidx = lax.broadcasted_iota(jnp.int32, (e, tb), 0).astype(F32)
    top_v, hots = [], []
    for _ in range(TOP_K):
        m = jnp.max(vals, axis=0, keepdims=True)
        idx = jnp.min(jnp.where(vals == m, eidx, float(e)), axis=0, keepdims=True)
        hot = eidx == idx
        vals = jnp.where(hot, -jnp.inf, vals)
        top_v.append(m)
        hots.append(hot)
    ex = [jnp.exp(v - top_v[0]) for v in top_v]
    den = ex[0] + ex[1] + ex[2] + ex[3]
    gates = [x / den for x in ex]
    multi = jnp.zeros((e, tb), F32)
    for hot in hots:
        multi = multi + jnp.where(hot, 1.0, 0.0)
    ri = lax.broadcasted_iota(jnp.int32, (tb, tb), 0)
    ci = lax.broadcasted_iota(jnp.int32, (tb, tb), 1)
    upper = jnp.where(ri <= ci, 1.0, 0.0).astype(BF16)
    incl = jnp.dot(multi.astype(BF16), upper, preferred_element_type=F32)
    cnt = jnp.broadcast_to(incl[:, tb - 1:tb], (e, LANES))
    cnt = jnp.ceil(cnt * (1.0 / STRIP_ALIGN)) * STRIP_ALIGN
    er = lax.broadcasted_iota(jnp.int32, (e, e), 0)
    ec = lax.broadcasted_iota(jnp.int32, (e, e), 1)
    lower = jnp.where(er > ec, 1.0, 0.0).astype(BF16)
    cnt_hi = jnp.floor(cnt * (1.0 / BF16_EXACT_INT)) * BF16_EXACT_INT
    cnt_lo = cnt - cnt_hi
    start = (jnp.dot(lower, cnt_hi.astype(BF16), preferred_element_type=F32)
             + jnp.dot(lower, cnt_lo.astype(BF16), preferred_element_type=F32))[:, 0:1]
    where_to = start + incl - multi
    lpos = [jnp.sum(jnp.where(hot, where_to, 0.0), axis=0, keepdims=True) for hot in hots]
    lpos_ref[...] = jnp.concatenate(lpos + lpos, axis=0).astype(jnp.int32)
    gate_ref[...] = jnp.concatenate(gates + [jnp.zeros_like(g) for g in gates], axis=0)
    cnt_ref[...] = cnt
    before_ref[...] = carry_ref[...]
    carry_ref[...] = carry_ref[...] + cnt


def _route(logits_t):
    e, t = logits_t.shape
    tb = min(MOE_TOKENS, t)
    nblk = t // tb
    return pl.pallas_call(
        _route_kernel,
        grid=(nblk,),
        in_specs=[pl.BlockSpec((e, tb), lambda i: (0, i))],
        out_specs=[
            pl.BlockSpec((2 * TOP_K, tb), lambda i: (0, i)),
            pl.BlockSpec((2 * TOP_K, tb), lambda i: (0, i)),
            pl.BlockSpec((None, e, LANES), lambda i: (i, 0, 0)),
            pl.BlockSpec((None, e, LANES), lambda i: (i, 0, 0)),
        ],
        out_shape=[
            jax.ShapeDtypeStruct((2 * TOP_K, t), jnp.int32),
            jax.ShapeDtypeStruct((2 * TOP_K, t), F32),
            jax.ShapeDtypeStruct((nblk, e, LANES), F32),
            jax.ShapeDtypeStruct((nblk, e, LANES), F32),
        ],
        scratch_shapes=[pltpu.VMEM((e, LANES), F32)],
        compiler_params=_params("arbitrary"),
        name="route",
    )(logits_t)


def _local_rows(tb):
    return TOP_K * tb + STRIP_ALIGN * N_EXPERTS


def _for_each_strip(cnt_ref, dst_ref, blk, fn):
    def per_expert(e, local_row):
        c = cnt_ref[blk * N_EXPERTS + e]
        g = dst_ref[blk * N_EXPERTS + e]

        @pl.when(c > 0)
        def _():
            fn(pl.multiple_of(local_row, STRIP_ALIGN), pl.multiple_of(g, STRIP_ALIGN), pl.multiple_of(c, STRIP_ALIGN))

        return local_row + c

    lax.fori_loop(0, N_EXPERTS, per_expert, jnp.int32(0))


def _dispatch_kernel(cnt_ref, dst_ref, tail_ref, x_ref, lpos_ref, xb_hbm, ls_ref, zero_ref, sem, zero_sem):
    i = pl.program_id(0)
    tb = x_ref.shape[0]
    n = ls_ref.shape[1]
    bm = zero_ref.shape[0]

    @pl.when(i == 0)
    def _():
        zero_ref[...] = jnp.zeros_like(zero_ref)
        n_blocks = xb_hbm.shape[0] // bm
        n_used = tail_ref[N_EXPERTS]

        def zero_copy(row):
            return pltpu.make_async_copy(zero_ref, xb_hbm.at[pl.ds(pl.multiple_of(row, STRIP_ALIGN), bm)], zero_sem)

        def start(e, c):
            @pl.when(tail_ref[e] >= 0)
            def _():
                zero_copy(tail_ref[e]).start()
            return c

        def wait(e, c):
            @pl.when(tail_ref[e] >= 0)
            def _():
                zero_copy(tail_ref[e]).wait()
            return c

        def start_unused(blk, c):
            zero_copy(blk * bm).start()
            return c

        def wait_unused(blk, c):
            zero_copy(blk * bm).wait()
            return c

        lax.fori_loop(0, N_EXPERTS, start, 0)
        lax.fori_loop(n_used, n_blocks, start_unused, 0)
        lax.fori_loop(0, N_EXPERTS, wait, 0)
        lax.fori_loop(n_used, n_blocks, wait_unused, 0)

    lp = lpos_ref[...].astype(jnp.int16)
    xb = x_ref[...].astype(BF16)
    slot = i % 2
    rc = n // MOE_CHUNKS
    one = jnp.ones((rc, tb), BF16)
    for r0 in range(0, n, rc):
        r = (r0 + lax.broadcasted_iota(jnp.int32, (rc, tb), 0)).astype(jnp.int16)
        perm = jnp.zeros((rc, tb), BF16)
        for k in range(TOP_K):
            perm = jnp.where(r == lp[k:k + 1], one, perm)
        ls_ref[slot, r0:r0 + rc, :] = jnp.dot(perm, xb, preferred_element_type=F32)

    def strips(blk, blk_slot, act):
        def piece(local_row, global_row, rows):
            cp = pltpu.make_async_copy(ls_ref.at[blk_slot, pl.ds(local_row, rows)],
                                       xb_hbm.at[pl.ds(global_row, rows)], sem.at[blk_slot])
            cp.start() if act == "start" else cp.wait()
        _for_each_strip(cnt_ref, dst_ref, blk, piece)

    strips(i, slot, "start")

    @pl.when(i > 0)
    def _():
        strips(i - 1, 1 - slot, "wait")

    @pl.when(i == pl.num_programs(0) - 1)
    def _():
        strips(i, slot, "wait")


def _dispatch(x1, lpos, cnt_flat, dst_flat, tails, n_pad):
    t, d = x1.shape
    tb = min(MOE_TOKENS, t)
    return pl.pallas_call(
        _dispatch_kernel,
        grid_spec=pltpu.PrefetchScalarGridSpec(
            num_scalar_prefetch=3,
            grid=(t // tb,),
            in_specs=[
                pl.BlockSpec((tb, d), lambda i, c, g, tl: (i, 0)),
                pl.BlockSpec((2 * TOP_K, tb), lambda i, c, g, tl: (0, i)),
            ],
            out_specs=pl.BlockSpec(memory_space=pl.ANY),
            scratch_shapes=[
                pltpu.VMEM((2, _local_rows(tb), d), F32),
                pltpu.VMEM((EXPERT_ROWS, d), F32),
                pltpu.SemaphoreType.DMA((2,)),
                pltpu.SemaphoreType.DMA,
            ],
        ),
        out_shape=jax.ShapeDtypeStruct((n_pad, d), F32),
        compiler_params=_params("arbitrary"),
        name="moe_dispatch",
    )(cnt_flat, dst_flat, tails, x1, lpos)


def _expert_kernel(be_ref, nu_ref, x_ref, wgu_ref, bgu_ref, wd_ref, bd_ref, y_ref, wgu_b, wd_b):
    i = pl.program_id(0)
    f = wd_ref.shape[0]

    @pl.when(i >= nu_ref[0])
    def _():
        y_ref[...] = jnp.zeros_like(y_ref)

    @pl.when(i < nu_ref[0])
    def _():
        prev = be_ref[jnp.maximum(i - 1, 0)]

        @pl.when((i == 0) | (be_ref[i] != prev))
        def _():
            wgu_b[...] = wgu_ref[...].astype(BF16)
            wd_b[...] = wd_ref[...].astype(BF16)

        xb = x_ref[...].astype(BF16)
        fc = EXPERT_FC
        acc = jnp.zeros(y_ref.shape, F32) + bd_ref[...]
        for j in range(0, f, fc):
            h_glu = jnp.dot(xb, wgu_b[:, j:j + fc], preferred_element_type=F32) + bgu_ref[:, j:j + fc]
            h_lin = (jnp.dot(xb, wgu_b[:, f + j:f + j + fc], preferred_element_type=F32)
                     + bgu_ref[:, f + j:f + j + fc])
            x_glu = jnp.minimum(h_glu, SWIGLU_LIMIT)
            x_lin = jnp.clip(h_lin, -SWIGLU_LIMIT, SWIGLU_LIMIT)
            act = x_glu * jax.nn.sigmoid(SWIGLU_ALPHA * x_glu) * (x_lin + 1.0)
            acc = acc + jnp.dot(act.astype(BF16), wd_b[j:j + fc, :], preferred_element_type=F32)
        y_ref[...] = acc


def _expert_ffn(xb, blk_e, n_used, w_gu, b_gu, w_down, b_down, layer):
    n_pad, d = xb.shape
    _, e, _, f2 = w_gu.shape
    f = f2 // 2
    bm = EXPERT_ROWS
    nb = n_pad // bm
    b_gu4 = b_gu.reshape(b_gu.shape[0], e, 1, f2)
    b_down4 = b_down.reshape(b_down.shape[0], e, 1, d)

    def row_map(i, be, nu):
        return (jnp.minimum(i, nu[0] - 1), 0)

    def w_map(i, be, nu):
        return (layer, be[i], 0, 0)

    return pl.pallas_call(
        _expert_kernel,
        grid_spec=pltpu.PrefetchScalarGridSpec(
            num_scalar_prefetch=2,
            grid=(nb,),
            in_specs=[
                pl.BlockSpec((bm, d), row_map),
                pl.BlockSpec((None, None, d, f2), w_map),
                pl.BlockSpec((None, None, 1, f2), w_map),
                pl.BlockSpec((None, None, f, d), w_map),
                pl.BlockSpec((None, None, 1, d), w_map),
            ],
            out_specs=pl.BlockSpec((bm, d), lambda i, be, nu: (i, 0)),
            scratch_shapes=[pltpu.VMEM((d, f2), BF16), pltpu.VMEM((f, d), BF16)],
        ),
        out_shape=jax.ShapeDtypeStruct((n_pad, d), F32),
        compiler_params=_params("arbitrary"),
        name="moe_experts",
    )(blk_e, n_used, xb, w_gu, b_gu4, w_down, b_down4)


def _combine_kernel(cnt_ref, dst_ref, x_ref, lpos_ref, gate_ref, g_ref, b_ref, yb_hbm, o_ref, ly_ref, sem):
    i = pl.program_id(0)
    tb = x_ref.shape[0]
    n = ly_ref.shape[1]
    slot = i % 2

    def strips(blk, blk_slot, act):
        def piece(local_row, global_row, rows):
            cp = pltpu.make_async_copy(yb_hbm.at[pl.ds(global_row, rows)],
                                       ly_ref.at[blk_slot, pl.ds(local_row, rows)], sem.at[blk_slot])
            cp.start() if act == "start" else cp.wait()
        _for_each_strip(cnt_ref, dst_ref, blk, piece)

    @pl.when(i == 0)
    def _():
        ly_ref[...] = jnp.zeros_like(ly_ref)
        strips(0, 0, "start")

    @pl.when(i + 1 < pl.num_programs(0))
    def _():
        strips(i + 1, 1 - slot, "start")

    strips(i, slot, "wait")

    cc = n // MOE_CHUNKS
    lp = lpos_ref[...].astype(jnp.int16)
    gate = gate_ref[...].astype(BF16)
    m = jnp.zeros(x_ref.shape, F32)
    for c0 in range(0, n, cc):
        c = (c0 + lax.broadcasted_iota(jnp.int32, (tb, cc), 1)).astype(jnp.int16)
        w = jnp.zeros((tb, cc), BF16)
        for k in range(TOP_K):
            w = jnp.where(c == lp[:, k:k + 1], gate[:, k:k + 1], w)
        m = m + jnp.dot(w, ly_ref[slot, c0:c0 + cc, :].astype(BF16), preferred_element_type=F32)
    o_ref[...] = _layer_norm(DEEPNORM_ALPHA * x_ref[...] + m, g_ref[...], b_ref[...])


def _combine(x1, lpos_t, gates_t, ln_g, ln_b, cnt_flat, dst_flat, yb):
    t, d = x1.shape
    tb = min(MOE_TOKENS, t)
    return pl.pallas_call(
        _combine_kernel,
        grid_spec=pltpu.PrefetchScalarGridSpec(
            num_scalar_prefetch=2,
            grid=(t // tb,),
            in_specs=[
                pl.BlockSpec((tb, d), lambda i, c, g: (i, 0)),
                pl.BlockSpec((tb, 2 * TOP_K), lambda i, c, g: (i, 0)),
                pl.BlockSpec((tb, 2 * TOP_K), lambda i, c, g: (i, 0)),
                pl.BlockSpec((1, d), lambda i, c, g: (0, 0)),
                pl.BlockSpec((1, d), lambda i, c, g: (0, 0)),
                pl.BlockSpec(memory_space=pl.ANY),
            ],
            out_specs=pl.BlockSpec((tb, d), lambda i, c, g: (i, 0)),
            scratch_shapes=[
                pltpu.VMEM((2, _local_rows(tb), d), F32),
                pltpu.SemaphoreType.DMA((2,)),
            ],
        ),
        out_shape=jax.ShapeDtypeStruct((t, d), F32),
        compiler_params=_params("arbitrary"),
        name="moe_combine",
    )(cnt_flat, dst_flat, x1, lpos_t, gates_t, ln_g, ln_b, yb)


def _moe(x1, logits, ln_g, ln_b, w_gu, b_gu, w_down, b_down, layer):
    t, d = x1.shape
    e = N_EXPERTS
    bm = EXPERT_ROWS
    lpos, gates, cnt_blk, before_blk = _route(logits[:, :e].T)
    cnt_blk = cnt_blk[:, :, 0].astype(jnp.int32)
    before_blk = before_blk[:, :, 0].astype(jnp.int32)
    cnt = before_blk[-1] + cnt_blk[-1]
    padded = (cnt + bm - 1) // bm * bm
    pad_end = jnp.cumsum(padded)
    pad_start = pad_end - padded
    dst_blk = pad_start[None, :] + before_blk
    nb = -(-(t * TOP_K + (STRIP_ALIGN - 1) * cnt_blk.shape[0] * e) // bm) + e
    n_used = (pad_end[-1] // bm).astype(jnp.int32)
    blk_start = jnp.arange(nb, dtype=jnp.int32) * bm
    blk_e = jnp.minimum(jnp.sum(pad_end[None, :] <= blk_start[:, None], axis=1), e - 1).astype(jnp.int32)
    blk_e = jnp.where(jnp.arange(nb) < n_used, blk_e, blk_e[jnp.maximum(n_used - 1, 0)])
    tails = jnp.concatenate([jnp.where(padded > 0, pad_end - bm, -1), n_used.reshape(1)]).astype(jnp.int32)
    cnt_flat = cnt_blk.reshape(-1)
    dst_flat = dst_blk.reshape(-1).astype(jnp.int32)
    xb = _dispatch(x1, lpos, cnt_flat, dst_flat, tails, nb * bm)
    yb = _expert_ffn(xb, blk_e, n_used.reshape(1), w_gu, b_gu, w_down, b_down, layer)
    return _combine(x1, lpos.T, gates.T, ln_g, ln_b, cnt_flat, dst_flat, yb)


def _qkv_rope_kernel(x_ref, w_ref, pos_ref, inv_ref, out_ref):
    xb = x_ref[...].astype(BF16)
    tm = x_ref.shape[0]
    n_rope = 2 * (2 * DIFF_HEADS * DIFF_HD)
    n_freq = DIFF_HD // 2
    groups = LANES // n_freq
    rows = tm // groups
    pos = pos_ref[...].astype(F32)
    lane_c = lax.broadcasted_iota(jnp.int32, (rows, LANES), 1)
    pos_c = pos[(groups - 1) * rows:groups * rows]
    for g in reversed(range(groups - 1)):
        pos_c = jnp.where(lane_c < (g + 1) * n_freq, pos[g * rows:(g + 1) * rows], pos_c)
    ang = pos_c * inv_ref[...]

    def spread(tbl):
        out = []
        for g in range(groups):
            y = pltpu.roll(tbl, LANES - g * n_freq, 1) if g else tbl
            width = n_freq
            while width < LANES:
                y = jnp.where(lane_c < width, y, pltpu.roll(y, width, 1))
                width *= 2
            out.append(y)
        return jnp.concatenate(out, axis=0)

    cos = spread(jnp.cos(ang))
    sin = spread(jnp.sin(ang))
    lane = lax.broadcasted_iota(jnp.int32, (tm, LANES), 1)
    first_half = (lane % DIFF_HD) < (DIFF_HD // 2)
    sin_signed = jnp.where(first_half, -sin, sin)
    q_scale = DIFF_HD ** -0.5 * math.log2(math.e)
    n = w_ref.shape[1]
    for j0 in range(0, n, MXU_COLS):
        wide = jnp.dot(xb, w_ref[:, j0:j0 + MXU_COLS], preferred_element_type=F32)
        for j in range(j0, j0 + MXU_COLS, LANES):
            c = wide[:, j - j0:j - j0 + LANES]
            if j < n_rope:
                rot = jnp.where(first_half, pltpu.roll(c, LANES - DIFF_HD // 2, 1), pltpu.roll(c, DIFF_HD // 2, 1))
                c = c * cos + rot * sin_signed
                if j < n_rope // 2:
                    c = c * q_scale
            out_ref[:, j:j + LANES] = c.astype(BF16)


def _qkv_rope(x2d, w_qkv, pos_col, inv_row):
    t, d = x2d.shape
    n = w_qkv.shape[1]
    tm = min(ROW_TILE, t)
    return pl.pallas_call(
        _qkv_rope_kernel,
        grid=(t // tm,),
        in_specs=[
            pl.BlockSpec((tm, d), lambda i: (i, 0)),
            pl.BlockSpec((d, n), lambda i: (0, 0)),
            pl.BlockSpec((tm, 1), lambda i: (i, 0)),
            pl.BlockSpec((1, LANES), lambda i: (0, 0)),
        ],
        out_specs=pl.BlockSpec((tm, n), lambda i: (i, 0)),
        out_shape=jax.ShapeDtypeStruct((t, n), BF16),
        compiler_params=_params("arbitrary"),
        name="qkv_rope",
    )(x2d, w_qkv, pos_col, inv_row)


def _diff_attn_kernel(q_ref, k_ref, v_ref, lam_ref, sw_ref, o_ref, qs_ref, m_ref, l_ref, acc_ref, bias_ref, *,
                      lambda_init):
    qi = pl.program_id(2)
    tq = q_ref.shape[0]
    tk = bias_ref.shape[2]

    @pl.when((pl.program_id(0) == 0) & (pl.program_id(1) == 0) & (qi == 0))
    def _():
        for d in range(tq // tk):
            qrow = lax.broadcasted_iota(jnp.int32, (2 * tq, tk), 0) % tq
            kcol = d * tk + lax.broadcasted_iota(jnp.int32, (2 * tq, tk), 1)
            bias_ref[d] = jnp.where(kcol <= qrow, 0.0, -jnp.inf)

    q = q_ref[...]
    lane = lax.broadcasted_iota(jnp.int32, q.shape, 1)
    zero = jnp.zeros_like(q)
    qs_ref[0:tq, :] = jnp.where(lane < DIFF_HD, q, zero)
    qs_ref[tq:2 * tq, :] = jnp.where(lane >= DIFF_HD, q, zero)
    m_ref[...] = jnp.full(m_ref.shape, -jnp.inf, F32)
    l_ref[...] = jnp.zeros_like(l_ref)
    acc_ref[...] = jnp.zeros_like(acc_ref)

    def block(kj, diag=None):
        k0 = pl.multiple_of(kj * tk, tk)
        s = _nt_dot(qs_ref[...], k_ref[pl.ds(k0, tk), :])
        if diag is not None:
            s = s + bias_ref[diag]
        m_old = m_ref[...]
        m_new = jnp.maximum(m_old, jnp.max(s, axis=-1, keepdims=True))
        p = jnp.exp2((s - jnp.concatenate([m_new] * (tk // LANES), axis=1)).astype(BF16))
        alpha = jnp.exp2(m_old - m_new)
        l_ref[...] = alpha * l_ref[...] + jnp.sum(p.astype(F32), axis=-1, keepdims=True)
        acc_ref[...] = alpha * acc_ref[...] + jnp.dot(p, v_ref[pl.ds(k0, tk), :], preferred_element_type=F32)
        m_ref[...] = m_new

    n_full = (qi * tq) // tk

    def pair(kp, c):
        block(2 * kp)
        block(2 * kp + 1)
        return c

    def single(kj, c):
        block(kj)
        return c

    lax.fori_loop(0, n_full // 2, pair, 0)
    lax.fori_loop((n_full // 2) * 2, n_full, single, 0)
    for d in range(tq // tk):
        block(n_full + d, d)

    lam = lam_ref[...]
    lam_full = (jnp.exp(jnp.sum(lam[0:1] * lam[1:2], axis=-1, keepdims=True))
                - jnp.exp(jnp.sum(lam[2:3] * lam[3:4], axis=-1, keepdims=True)) + lambda_init)
    o1 = acc_ref[0:tq, :] / l_ref[0:tq, :]
    o2 = acc_ref[tq:2 * tq, :] / l_ref[tq:2 * tq, :]
    o = o1 - lam_full * o2
    o = o * lax.rsqrt(jnp.mean(o * o, axis=-1, keepdims=True) + RMS_EPS) * sw_ref[...]
    o_ref[...] = (o * (1.0 - lambda_init)).astype(BF16)


def _diff_attn(qkv, lam, subln_w, b, s, lambda_init):
    t = b * s
    h, w = DIFF_HEADS, 2 * DIFF_HD
    tq = min(ATTN_Q, s)
    tk = min(ATTN_K, tq)
    assert tq % tk == 0
    nq = s // tq
    return pl.pallas_call(
        functools.partial(_diff_attn_kernel, lambda_init=lambda_init),
        grid=(b, h, nq),
        in_specs=[
            pl.BlockSpec((tq, w), lambda bi, hi, qi: (bi * nq + qi, hi)),
            pl.BlockSpec((s, w), lambda bi, hi, qi: (bi, h + hi)),
            pl.BlockSpec((s, DIFF_VD), lambda bi, hi, qi: (bi, 2 * h + hi)),
            pl.BlockSpec((4, DIFF_HD), lambda bi, hi, qi: (0, 0)),
            pl.BlockSpec((1, DIFF_VD), lambda bi, hi, qi: (0, 0)),
        ],
        out_specs=pl.BlockSpec((tq, DIFF_VD), lambda bi, hi, qi: (bi * nq + qi, hi)),
        out_shape=jax.ShapeDtypeStruct((t, h * DIFF_VD), BF16),
        scratch_shapes=[
            pltpu.VMEM((2 * tq, w), BF16),
            pltpu.VMEM((2 * tq, LANES), F32),
            pltpu.VMEM((2 * tq, LANES), F32),
            pltpu.VMEM((2 * tq, DIFF_VD), F32),
            pltpu.VMEM((tq // tk, 2 * tq, tk), F32),
        ],
        compiler_params=_params("arbitrary", "arbitrary", "arbitrary"),
        name="diff_attn",
    )(qkv, qkv, qkv, lam, subln_w)


def _pad_cols(a, n):
    return jnp.pad(a, ((0, 0), (0, n - a.shape[1])))


def kernel(x, positions, a_w_in, a_w_gk2, a_b_gk, a_norm_w, a_w_out, kv_w, b_w_q, b_lambda, b_subln_w,
           b_w_out, ln1_g, ln1_b, ln2_g, ln2_b, router_w, router_b, moe_w_gu, moe_b_gu, moe_w_down,
           moe_b_down):
    b, s, d = x.shape
    t = b * s
    xc = x.reshape(t, d)
    n_main = 2 * GLA_HEADS * GLA_DK + 2 * GLA_HEADS * GLA_DV
    half = DIFF_HD // 2
    inv = ROPE_THETA ** (-jnp.arange(half, dtype=F32) * 2.0 / DIFF_HD)
    inv_row = jnp.tile(inv, LANES // half).reshape(1, LANES)
    pos_col = positions.reshape(t, 1)
    assert DEPTH - N_A_LAYERS == 1
    for l in range(DEPTH):
        if l < N_A_LAYERS:
            w_in = a_w_in[l]
            qkvr, logg = _gla_proj(
                xc, w_in[:, :n_main].astype(BF16), _pad_cols(w_in[:, n_main:], LANES).astype(BF16),
                jnp.pad(a_w_gk2[l], ((0, LANES - GLA_GATE_RANK), (0, 0))), a_b_gk[l].reshape(1, -1))
            o = _gla_core(qkvr, logg, a_norm_w[l].reshape(1, -1), b, s)
            w_out = a_w_out[l]
        else:
            j = l - N_A_LAYERS
            lambda_init = 0.8 - 0.6 * math.exp(-0.3 * l)
            w_qkv = jnp.concatenate([b_w_q[j], kv_w], axis=1).astype(BF16)
            qkv = _qkv_rope(xc, w_qkv, pos_col, inv_row)
            o = _diff_attn(qkv, b_lambda[j], b_subln_w[j].reshape(1, -1), b, s, lambda_init)
            w_out = b_w_out[j]
        rw = _pad_cols(router_w[l], LANES)
        rw_hi = rw.astype(BF16)
        rw_lo = (rw - rw_hi.astype(F32)).astype(BF16)
        x1, logits = _proj_ln(o, w_out.astype(BF16), xc, ln1_g[l].reshape(1, -1), ln1_b[l].reshape(1, -1),
                              jnp.concatenate([rw_hi, rw_lo], axis=1), _pad_cols(router_b[l].reshape(1, -1), LANES))
        xc = _moe(x1, logits, ln2_g[l].reshape(1, -1), ln2_b[l].reshape(1, -1),
                  moe_w_gu, moe_b_gu, moe_w_down, moe_b_down, l)
    return xc.reshape(b, s, d)
```

```python
import functools
import math

import jax
import jax.numpy as jnp
from jax import lax
from jax.experimental import pallas as pl
from jax.experimental.pallas import tpu as pltpu

F32 = jnp.float32
BF16 = jnp.bfloat16
HIGHEST = lax.Precision.HIGHEST

DEPTH = 2
N_A_LAYERS = DEPTH // 2
GLA_HEADS = 4
GLA_DK = 128
GLA_DV = 256
GLA_GATE_RANK = 16
GLA_GATE_TAU = 16.0
GLA_CHUNK = 64
DIFF_HEADS = 8
DIFF_HD = 64
DIFF_VD = 128
ROPE_THETA = 10000.0
N_EXPERTS = 32
TOP_K = 4
SWIGLU_LIMIT = 7.0
SWIGLU_ALPHA = 1.702
LN_EPS = 1e-5
RMS_EPS = 1e-5
DEEPNORM_ALPHA = (2.0 * DEPTH) ** 0.25

LANES = 128
SUBLANES = 8
MXU_COLS = 256
BF16_EXACT_INT = 256
STRIP_ALIGN = SUBLANES
VMEM_LIMIT_BYTES = 56 * 1024 * 1024

ROW_TILE = 512
EXPERT_ROWS = 512
EXPERT_FC = 512
MOE_TOKENS = 512
MOE_CHUNKS = 3
ATTN_Q = 512
ATTN_K = 512
ATTN_GROUPS = (4, 2, 1)
GLA_ROWS = 1024


def _params(*sem):
    return pltpu.CompilerParams(dimension_semantics=sem, vmem_limit_bytes=VMEM_LIMIT_BYTES)


def _nt_dot(a, b):
    return lax.dot_general(a, b, (((1,), (1,)), ((), ())), preferred_element_type=F32)


def _gla_proj_kernel(x_ref, w_ref, wg_ref, wgk2_ref, bgk_ref, qkvr_ref, logg_ref):
    xb = x_ref[...].astype(BF16)
    n = w_ref.shape[1]
    for j in range(0, n, 512):
        qkvr_ref[:, j:j + 512] = jnp.dot(xb, w_ref[:, j:j + 512],
                                         preferred_element_type=F32).astype(BF16)
    gk_low = jnp.dot(xb, wg_ref[...], preferred_element_type=F32)
    z = jnp.dot(gk_low, wgk2_ref[...], preferred_element_type=F32, precision=HIGHEST) + bgk_ref[...]
    logg_ref[...] = (jnp.minimum(z, 0.0) - jnp.log1p(jnp.exp(-jnp.abs(z)))) / GLA_GATE_TAU


def _gla_proj(x2d, w_main, w_gate, w_gk2, b_gk):
    t, d = x2d.shape
    n = w_main.shape[1]
    hk = b_gk.shape[1]
    tm = min(ROW_TILE, t)
    return pl.pallas_call(
        _gla_proj_kernel,
        grid=(t // tm,),
        in_specs=[
            pl.BlockSpec((tm, d), lambda i: (i, 0)),
            pl.BlockSpec((d, n), lambda i: (0, 0)),
            pl.BlockSpec((d, LANES), lambda i: (0, 0)),
            pl.BlockSpec((LANES, hk), lambda i: (0, 0)),
            pl.BlockSpec((1, hk), lambda i: (0, 0)),
        ],
        out_specs=[
            pl.BlockSpec((tm, n), lambda i: (i, 0)),
            pl.BlockSpec((tm, hk), lambda i: (i, 0)),
        ],
        out_shape=[jax.ShapeDtypeStruct((t, n), BF16), jax.ShapeDtypeStruct((t, hk), F32)],
        compiler_params=_params("arbitrary"),
        name="gla_proj",
    )(x2d, w_main, w_gate, w_gk2, b_gk)


def _gla_kernel(q_ref, k_ref, v_ref, r_ref, lg_ref, nw_ref, o_ref, st_ref):
    c = GLA_CHUNK

    @pl.when(pl.program_id(2) == 0)
    def _():
        st_ref[...] = jnp.zeros_like(st_ref)

    row = lax.broadcasted_iota(jnp.int32, (c, c), 0)
    col = lax.broadcasted_iota(jnp.int32, (c, c), 1)
    tril = row >= col
    row_id = lax.broadcasted_iota(jnp.int32, (c, GLA_DK), 0)
    n_chunks = q_ref.shape[0] // c

    st = st_ref[...]
    for ci in range(n_chunks):
        r0 = ci * c
        lg = lg_ref[pl.ds(r0, c), :]
        g = lg
        shift = 1
        while shift < c:
            g = g + jnp.where(row_id >= shift, pltpu.roll(g, shift, 0), 0.0)
            shift *= 2
        g_last = g[c - 1:c, :]
        q = q_ref[pl.ds(r0, c), :].astype(F32) * (GLA_DK ** -0.5)
        k = k_ref[pl.ds(r0, c), :].astype(F32)
        v = v_ref[pl.ds(r0, c), :]
        q_in = (q * jnp.exp(g)).astype(BF16)
        k_in = (k * jnp.exp(-g)).astype(BF16)
        k_state = (k * jnp.exp(g_last - g)).astype(BF16)
        decay = jnp.exp(g_last)
        scores = jnp.where(tril, _nt_dot(q_in, k_in), 0.0).astype(BF16)
        o = jnp.dot(scores, v, preferred_element_type=F32) + _nt_dot(q_in, st.astype(BF16))
        v_t = v.astype(F32).T.astype(BF16)
        st = st * decay + jnp.dot(v_t, k_state, preferred_element_type=F32)
        o = o * lax.rsqrt(jnp.mean(o * o, axis=-1, keepdims=True) + RMS_EPS) * nw_ref[...]
        r = r_ref[pl.ds(r0, c), :].astype(F32)
        o_ref[pl.ds(r0, c), :] = (o * (r * jax.nn.sigmoid(r))).astype(BF16)
    st_ref[...] = st


def _gla_core(qkvr, logg, norm_w, b, s):
    t = b * s
    h, dk, dv = GLA_HEADS, GLA_DK, GLA_DV
    ts = min(GLA_ROWS, s)
    ns = s // ts
    k_off = (h * dk) // dk
    v_off = (2 * h * dk) // dv
    r_off = (2 * h * dk + h * dv) // dv
    return pl.pallas_call(
        _gla_kernel,
        grid=(b, h, ns),
        in_specs=[
            pl.BlockSpec((ts, dk), lambda bi, hi, si: (bi * ns + si, hi)),
            pl.BlockSpec((ts, dk), lambda bi, hi, si: (bi * ns + si, k_off + hi)),
            pl.BlockSpec((ts, dv), lambda bi, hi, si: (bi * ns + si, v_off + hi)),
            pl.BlockSpec((ts, dv), lambda bi, hi, si: (bi * ns + si, r_off + hi)),
            pl.BlockSpec((ts, dk), lambda bi, hi, si: (bi * ns + si, hi)),
            pl.BlockSpec((1, dv), lambda bi, hi, si: (0, 0)),
        ],
        out_specs=pl.BlockSpec((ts, dv), lambda bi, hi, si: (bi * ns + si, hi)),
        out_shape=jax.ShapeDtypeStruct((t, h * dv), BF16),
        scratch_shapes=[pltpu.VMEM((dv, dk), F32)],
        compiler_params=_params("arbitrary", "arbitrary", "arbitrary"),
        name="gla_core",
    )(qkvr, qkvr, qkvr, qkvr, logg, norm_w)


def _layer_norm(y, g, b):
    mu = jnp.mean(y, axis=-1, keepdims=True)
    yc = y - mu
    var = jnp.mean(yc * yc, axis=-1, keepdims=True)
    return yc * lax.rsqrt(var + LN_EPS) * g + b


def _proj_ln_kernel(o_ref, w_ref, x_ref, g_ref, b_ref, rw_ref, rb_ref, x1_ref, logit_ref):
    h = jnp.dot(o_ref[...], w_ref[...], preferred_element_type=F32)
    x1 = _layer_norm(DEEPNORM_ALPHA * x_ref[...] + h, g_ref[...], b_ref[...])
    x1_ref[...] = x1
    xh = x1.astype(BF16)
    xl = (x1 - xh.astype(F32)).astype(BF16)
    hw = jnp.dot(xh, rw_ref[...], preferred_element_type=F32)
    lw = jnp.dot(xl, rw_ref[:, 0:LANES], preferred_element_type=F32)
    logit_ref[...] = hw[:, 0:LANES] + hw[:, LANES:2 * LANES] + lw + rb_ref[...]


def _proj_ln(o, w_out, x2d, ln_g, ln_b, rw, rb):
    t, d = x2d.shape
    kdim = o.shape[1]
    tm = min(ROW_TILE, t)
    return pl.pallas_call(
        _proj_ln_kernel,
        grid=(t // tm,),
        in_specs=[
            pl.BlockSpec((tm, kdim), lambda i: (i, 0)),
            pl.BlockSpec((kdim, d), lambda i: (0, 0)),
            pl.BlockSpec((tm, d), lambda i: (i, 0)),
            pl.BlockSpec((1, d), lambda i: (0, 0)),
            pl.BlockSpec((1, d), lambda i: (0, 0)),
            pl.BlockSpec((d, 2 * LANES), lambda i: (0, 0)),
            pl.BlockSpec((1, LANES), lambda i: (0, 0)),
        ],
        out_specs=[
            pl.BlockSpec((tm, d), lambda i: (i, 0)),
            pl.BlockSpec((tm, LANES), lambda i: (i, 0)),
        ],
        out_shape=[jax.ShapeDtypeStruct((t, d), F32), jax.ShapeDtypeStruct((t, LANES), F32)],
        compiler_params=_params("arbitrary"),
        name="proj_ln",
    )(o, w_out, x2d, ln_g, ln_b, rw, rb)


def _route_kernel(lt_ref, lpos_ref, gate_ref, cnt_ref, before_ref, carry_ref):
    e, tb = lt_ref.shape

    @pl.when(pl.program_id(0) == 0)
    def _():
        carry_ref[...] = jnp.zeros_like(carry_ref)

    vals = lt_ref[...]
    eidx = lax.broadcasted_iota(jnp.int32, (e, tb), 0).astype(F32)
    top_v, hots = [], []
    for _ in range(TOP_K):
        m = jnp.max(vals, axis=0, keepdims=True)
        idx = jnp.min(jnp.where(vals == m, eidx, float(e)), axis=0, keepdims=True)
        hot = eidx == idx
        vals = jnp.where(hot, -jnp.inf, vals)
        top_v.append(m)
        hots.append(hot)
    ex = [jnp.exp(v - top_v[0]) for v in top_v]
    den = ex[0] + ex[1] + ex[2] + ex[3]
    gates = [x / den for x in ex]
    multi = jnp.zeros((e, tb), F32)
    for hot in hots:
        multi = multi + jnp.where(hot, 1.0, 0.0)
    ri = lax.broadcasted_iota(jnp.int32, (tb, tb), 0)
    ci = lax.broadcasted_iota(jnp.int32, (tb, tb), 1)
    upper = jnp.where(ri <= ci, 1.0, 0.0).astype(BF16)
    incl = jnp.dot(multi.astype(BF16), upper, preferred_element_type=F32)
    cnt = jnp.broadcast_to(incl[:, tb - 1:tb], (e, LANES))
    cnt = jnp.ceil(cnt * (1.0 / STRIP_ALIGN)) * STRIP_ALIGN
    er = lax.broadcasted_iota(jnp.int32, (e, e), 0)
    ec = lax.broadcasted_iota(jnp.int32, (e, e), 1)
    lower = jnp.where(er > ec, 1.0, 0.0).astype(BF16)
    cnt_hi = jnp.floor(cnt * (1.0 / BF16_EXACT_INT)) * BF16_EXACT_INT
    cnt_lo = cnt - cnt_hi
    start = (jnp.dot(lower, cnt_hi.astype(BF16), preferred_element_type=F32)
             + jnp.dot(lower, cnt_lo.astype(BF16), preferred_element_type=F32))[:, 0:1]
    where_to = start + incl - multi
    lpos = [jnp.sum(jnp.where(hot, where_to, 0.0), axis=0, keepdims=True) for hot in hots]
    lpos_ref[...] = jnp.concatenate(lpos + lpos, axis=0).astype(jnp.int32)
    gate_ref[...] = jnp.concatenate(gates + [jnp.zeros_like(g) for g in gates], axis=0)
    cnt_ref[...] = cnt
    before_ref[...] = carry_ref[...]
    carry_ref[...] = carry_ref[...] + cnt


def _route(logits_t):
    e, t = logits_t.shape
    tb = min(MOE_TOKENS, t)
    nblk = t // tb
    return pl.pallas_call(
        _route_kernel,
        grid=(nblk,),
        in_specs=[pl.BlockSpec((e, tb), lambda i: (0, i))],
        out_specs=[
            pl.BlockSpec((2 * TOP_K, tb), lambda i: (0, i)),
            pl.BlockSpec((2 * TOP_K, tb), lambda i: (0, i)),
            pl.BlockSpec((None, e, LANES), lambda i: (i, 0, 0)),
            pl.BlockSpec((None, e, LANES), lambda i: (i, 0, 0)),
        ],
        out_shape=[
            jax.ShapeDtypeStruct((2 * TOP_K, t), jnp.int32),
            jax.ShapeDtypeStruct((2 * TOP_K, t), F32),
            jax.ShapeDtypeStruct((nblk, e, LANES), F32),
            jax.ShapeDtypeStruct((nblk, e, LANES), F32),
        ],
        scratch_shapes=[pltpu.VMEM((e, LANES), F32)],
        compiler_params=_params("arbitrary"),
        name="route",
    )(logits_t)


def _local_rows(tb):
    return TOP_K * tb + STRIP_ALIGN * N_EXPERTS


def _for_each_strip(cnt_ref, dst_ref, blk, fn):
    def per_expert(e, local_row):
        c = cnt_ref[blk * N_EXPERTS + e]
        g = dst_ref[blk * N_EXPERTS + e]

        @pl.when(c > 0)
        def _():
            fn(pl.multiple_of(local_row, STRIP_ALIGN), pl.multiple_of(g, STRIP_ALIGN), pl.multiple_of(c, STRIP_ALIGN))

        return local_row + c

    lax.fori_loop(0, N_EXPERTS, per_expert, jnp.int32(0))


def _dispatch_kernel(cnt_ref, dst_ref, tail_ref, x_ref, lpos_ref, xb_hbm, ls_ref, zero_ref, sem, zero_sem):
    i = pl.program_id(0)
    tb = x_ref.shape[0]
    n = ls_ref.shape[1]
    bm = zero_ref.shape[0]

    @pl.when(i == 0)
    def _():
        zero_ref[...] = jnp.zeros_like(zero_ref)
        n_blocks = xb_hbm.shape[0] // bm
        n_used = tail_ref[N_EXPERTS]

        def zero_copy(row):
            return pltpu.make_async_copy(zero_ref, xb_hbm.at[pl.ds(pl.multiple_of(row, STRIP_ALIGN), bm)], zero_sem)

        def start(e, c):
            @pl.when(tail_ref[e] >= 0)
            def _():
                zero_copy(tail_ref[e]).start()
            return c

        def wait(e, c):
            @pl.when(tail_ref[e] >= 0)
            def _():
                zero_copy(tail_ref[e]).wait()
            return c

        def start_unused(blk, c):
            zero_copy(blk * bm).start()
            return c

        def wait_unused(blk, c):
            zero_copy(blk * bm).wait()
            return c

        lax.fori_loop(0, N_EXPERTS, start, 0)
        lax.fori_loop(n_used, n_blocks, start_unused, 0)
        lax.fori_loop(0, N_EXPERTS, wait, 0)
        lax.fori_loop(n_used, n_blocks, wait_unused, 0)

    lp = lpos_ref[...].astype(jnp.int16)
    xb = x_ref[...].astype(BF16)
    slot = i % 2
    rc = n // MOE_CHUNKS
    one = jnp.ones((rc, tb), BF16)
    for r0 in range(0, n, rc):
        r = (r0 + lax.broadcasted_iota(jnp.int32, (rc, tb), 0)).astype(jnp.int16)
        perm = jnp.zeros((rc, tb), BF16)
        for k in range(TOP_K):
            perm = jnp.where(r == lp[k:k + 1], one, perm)
        ls_ref[slot, r0:r0 + rc, :] = jnp.dot(perm, xb, preferred_element_type=F32)

    def strips(blk, blk_slot, act):
        def piece(local_row, global_row, rows):
            cp = pltpu.make_async_copy(ls_ref.at[blk_slot, pl.ds(local_row, rows)],
                                       xb_hbm.at[pl.ds(global_row, rows)], sem.at[blk_slot])
            cp.start() if act == "start" else cp.wait()
        _for_each_strip(cnt_ref, dst_ref, blk, piece)

    strips(i, slot, "start")

    @pl.when(i > 0)
    def _():
        strips(i - 1, 1 - slot, "wait")

    @pl.when(i == pl.num_programs(0) - 1)
    def _():
        strips(i, slot, "wait")


def _dispatch(x1, lpos, cnt_flat, dst_flat, tails, n_pad):
    t, d = x1.shape
    tb = min(MOE_TOKENS, t)
    return pl.pallas_call(
        _dispatch_kernel,
        grid_spec=pltpu.PrefetchScalarGridSpec(
            num_scalar_prefetch=3,
            grid=(t // tb,),
            in_specs=[
                pl.BlockSpec((tb, d), lambda i, c, g, tl: (i, 0)),
                pl.BlockSpec((2 * TOP_K, tb), lambda i, c, g, tl: (0, i)),
            ],
            out_specs=pl.BlockSpec(memory_space=pl.ANY),
            scratch_shapes=[
                pltpu.VMEM((2, _local_rows(tb), d), F32),
                pltpu.VMEM((EXPERT_ROWS, d), F32),
                pltpu.SemaphoreType.DMA((2,)),
                pltpu.SemaphoreType.DMA,
            ],
        ),
        out_shape=jax.ShapeDtypeStruct((n_pad, d), F32),
        compiler_params=_params("arbitrary"),
        name="moe_dispatch",
    )(cnt_flat, dst_flat, tails, x1, lpos)


def _expert_kernel(be_ref, nu_ref, rows_ref, x_ref, wgu_ref, bgu_ref, wd_ref, bd_ref, y_ref, wgu_b, wd_b):
    i = pl.program_id(0)
    f = wd_ref.shape[0]
    bm = x_ref.shape[0]
    half = bm // 2

    def ffn(r0, rows):
        xb = x_ref[r0:r0 + rows, :].astype(BF16)
        fc = EXPERT_FC
        acc = jnp.zeros((rows, y_ref.shape[1]), F32) + bd_ref[...]
        for j in range(0, f, fc):
            h_glu = jnp.dot(xb, wgu_b[:, j:j + fc], preferred_element_type=F32) + bgu_ref[:, j:j + fc]
            h_lin = (jnp.dot(xb, wgu_b[:, f + j:f + j + fc], preferred_element_type=F32)
                     + bgu_ref[:, f + j:f + j + fc])
            x_glu = jnp.minimum(h_glu, SWIGLU_LIMIT)
            x_lin = jnp.clip(h_lin, -SWIGLU_LIMIT, SWIGLU_LIMIT)
            act = x_glu * jax.nn.sigmoid(SWIGLU_ALPHA * x_glu) * (x_lin + 1.0)
            acc = acc + jnp.dot(act.astype(BF16), wd_b[j:j + fc, :], preferred_element_type=F32)
        y_ref[r0:r0 + rows, :] = acc

    @pl.when(i >= nu_ref[0])
    def _():
        y_ref[...] = jnp.zeros_like(y_ref)

    @pl.when(i < nu_ref[0])
    def _():
        prev = be_ref[jnp.maximum(i - 1, 0)]

        @pl.when((i == 0) | (be_ref[i] != prev))
        def _():
            wgu_b[...] = wgu_ref[...].astype(BF16)
            wd_b[...] = wd_ref[...].astype(BF16)

        @pl.when(rows_ref[i] > half)
        def _():
            ffn(0, bm)

        @pl.when(rows_ref[i] <= half)
        def _():
            ffn(0, half)
            y_ref[half:bm, :] = jnp.zeros((bm - half, y_ref.shape[1]), F32)


def _expert_ffn(xb, blk_e, n_used, blk_rows, w_gu, b_gu, w_down, b_down, layer):
    n_pad, d = xb.shape
    _, e, _, f2 = w_gu.shape
    f = f2 // 2
    bm = EXPERT_ROWS
    nb = n_pad // bm
    b_gu4 = b_gu.reshape(b_gu.shape[0], e, 1, f2)
    b_down4 = b_down.reshape(b_down.shape[0], e, 1, d)

    def row_map(i, be, nu, rows):
        return (jnp.minimum(i, nu[0] - 1), 0)

    def w_map(i, be, nu, rows):
        return (layer, be[i], 0, 0)

    return pl.pallas_call(
        _expert_kernel,
        grid_spec=pltpu.PrefetchScalarGridSpec(
            num_scalar_prefetch=3,
            grid=(nb,),
            in_specs=[
                pl.BlockSpec((bm, d), row_map),
                pl.BlockSpec((None, None, d, f2), w_map),
                pl.BlockSpec((None, None, 1, f2), w_map),
                pl.BlockSpec((None, None, f, d), w_map),
                pl.BlockSpec((None, None, 1, d), w_map),
            ],
            out_specs=pl.BlockSpec((bm, d), lambda i, be, nu, rows: (i, 0)),
            scratch_shapes=[pltpu.VMEM((d, f2), BF16), pltpu.VMEM((f, d), BF16)],
        ),
        out_shape=jax.ShapeDtypeStruct((n_pad, d), F32),
        compiler_params=_params("arbitrary"),
        name="moe_experts",
    )(blk_e, n_used, blk_rows, xb, w_gu, b_gu4, w_down, b_down4)


def _combine_kernel(cnt_ref, dst_ref, x_ref, lpos_ref, gate_ref, g_ref, b_ref, yb_hbm, o_ref, ly_ref, sem):
    i = pl.program_id(0)
    tb = x_ref.shape[0]
    n = ly_ref.shape[1]
    slot = i % 2

    def strips(blk, blk_slot, act):
        def piece(local_row, global_row, rows):
            cp = pltpu.make_async_copy(yb_hbm.at[pl.ds(global_row, rows)],
                                       ly_ref.at[blk_slot, pl.ds(local_row, rows)], sem.at[blk_slot])
            cp.start() if act == "start" else cp.wait()
        _for_each_strip(cnt_ref, dst_ref, blk, piece)

    @pl.when(i == 0)
    def _():
        ly_ref[...] = jnp.zeros_like(ly_ref)
        strips(0, 0, "start")

    @pl.when(i + 1 < pl.num_programs(0))
    def _():
        strips(i + 1, 1 - slot, "start")

    strips(i, slot, "wait")

    cc = n // MOE_CHUNKS
    lp = lpos_ref[...].astype(jnp.int16)
    gate = gate_ref[...].astype(BF16)
    m = jnp.zeros(x_ref.shape, F32)
    for c0 in range(0, n, cc):
        c = (c0 + lax.broadcasted_iota(jnp.int32, (tb, cc), 1)).astype(jnp.int16)
        w = jnp.zeros((tb, cc), BF16)
        for k in range(TOP_K):
            w = jnp.where(c == lp[:, k:k + 1], gate[:, k:k + 1], w)
        m = m + jnp.dot(w, ly_ref[slot, c0:c0 + cc, :].astype(BF16), preferred_element_type=F32)
    o_ref[...] = _layer_norm(DEEPNORM_ALPHA * x_ref[...] + m, g_ref[...], b_ref[...])


def _combine(x1, lpos_t, gates_t, ln_g, ln_b, cnt_flat, dst_flat, yb):
    t, d = x1.shape
    tb = min(MOE_TOKENS, t)
    return pl.pallas_call(
        _combine_kernel,
        grid_spec=pltpu.PrefetchScalarGridSpec(
            num_scalar_prefetch=2,
            grid=(t // tb,),
            in_specs=[
                pl.BlockSpec((tb, d), lambda i, c, g: (i, 0)),
                pl.BlockSpec((tb, 2 * TOP_K), lambda i, c, g: (i, 0)),
                pl.BlockSpec((tb, 2 * TOP_K), lambda i, c, g: (i, 0)),
                pl.BlockSpec((1, d), lambda i, c, g: (0, 0)),
                pl.BlockSpec((1, d), lambda i, c, g: (0, 0)),
                pl.BlockSpec(memory_space=pl.ANY),
            ],
            out_specs=pl.BlockSpec((tb, d), lambda i, c, g: (i, 0)),
            scratch_shapes=[
                pltpu.VMEM((2, _local_rows(tb), d), F32),
                pltpu.SemaphoreType.DMA((2,)),
            ],
        ),
        out_shape=jax.ShapeDtypeStruct((t, d), F32),
        compiler_params=_params("arbitrary"),
        name="moe_combine",
    )(cnt_flat, dst_flat, x1, lpos_t, gates_t, ln_g, ln_b, yb)


def _moe(x1, logits, ln_g, ln_b, w_gu, b_gu, w_down, b_down, layer):
    t, d = x1.shape
    e = N_EXPERTS
    bm = EXPERT_ROWS
    lpos, gates, cnt_blk, before_blk = _route(logits[:, :e].T)
    cnt_blk = cnt_blk[:, :, 0].astype(jnp.int32)
    before_blk = before_blk[:, :, 0].astype(jnp.int32)
    cnt = before_blk[-1] + cnt_blk[-1]
    padded = (cnt + bm - 1) // bm * bm
    pad_end = jnp.cumsum(padded)
    pad_start = pad_end - padded
    dst_blk = pad_start[None, :] + before_blk
    nb = -(-(t * TOP_K + (STRIP_ALIGN - 1) * cnt_blk.shape[0] * e) // bm) + e
    n_used = (pad_end[-1] // bm).astype(jnp.int32)
    blk_start = jnp.arange(nb, dtype=jnp.int32) * bm
    blk_e = jnp.minimum(jnp.sum(pad_end[None, :] <= blk_start[:, None], axis=1), e - 1).astype(jnp.int32)
    blk_e = jnp.where(jnp.arange(nb) < n_used, blk_e, blk_e[jnp.maximum(n_used - 1, 0)])
    tails = jnp.concatenate([jnp.where(padded > 0, pad_end - bm, -1), n_used.reshape(1)]).astype(jnp.int32)
    cnt_flat = cnt_blk.reshape(-1)
    dst_flat = dst_blk.reshape(-1).astype(jnp.int32)
    xb = _dispatch(x1, lpos, cnt_flat, dst_flat, tails, nb * bm)
    blk_rows = jnp.clip((pad_start + cnt)[blk_e] - blk_start, 0, bm).astype(jnp.int32)
    yb = _expert_ffn(xb, blk_e, n_used.reshape(1), blk_rows, w_gu, b_gu, w_down, b_down, layer)
    return _combine(x1, lpos.T, gates.T, ln_g, ln_b, cnt_flat, dst_flat, yb)


def _qkv_rope_kernel(x_ref, w_ref, pos_ref, inv_ref, out_ref):
    xb = x_ref[...].astype(BF16)
    tm = x_ref.shape[0]
    n_rope = 2 * (2 * DIFF_HEADS * DIFF_HD)
    n_freq = DIFF_HD // 2
    groups = LANES // n_freq
    rows = tm // groups
    pos = pos_ref[...].astype(F32)
    lane_c = lax.broadcasted_iota(jnp.int32, (rows, LANES), 1)
    pos_c = pos[(groups - 1) * rows:groups * rows]
    for g in reversed(range(groups - 1)):
        pos_c = jnp.where(lane_c < (g + 1) * n_freq, pos[g * rows:(g + 1) * rows], pos_c)
    ang = pos_c * inv_ref[...]

    def spread(tbl):
        out = []
        for g in range(groups):
            y = pltpu.roll(tbl, LANES - g * n_freq, 1) if g else tbl
            width = n_freq
            while width < LANES:
                y = jnp.where(lane_c < width, y, pltpu.roll(y, width, 1))
                width *= 2
            out.append(y)
        return jnp.concatenate(out, axis=0)

    cos = spread(jnp.cos(ang))
    sin = spread(jnp.sin(ang))
    lane = lax.broadcasted_iota(jnp.int32, (tm, LANES), 1)
    first_half = (lane % DIFF_HD) < (DIFF_HD // 2)
    sin_signed = jnp.where(first_half, -sin, sin)
    q_scale = DIFF_HD ** -0.5 * math.log2(math.e)
    n = w_ref.shape[1]
    for j0 in range(0, n, MXU_COLS):
        wide = jnp.dot(xb, w_ref[:, j0:j0 + MXU_COLS], preferred_element_type=F32)
        for j in range(j0, j0 + MXU_COLS, LANES):
            c = wide[:, j - j0:j - j0 + LANES]
            if j < n_rope:
                rot = jnp.where(first_half, pltpu.roll(c, LANES - DIFF_HD // 2, 1), pltpu.roll(c, DIFF_HD // 2, 1))
                c = c * cos + rot * sin_signed
                if j < n_rope // 2:
                    c = c * q_scale
            out_ref[:, j:j + LANES] = c.astype(BF16)


def _qkv_rope(x2d, w_qkv, pos_col, inv_row):
    t, d = x2d.shape
    n = w_qkv.shape[1]
    tm = min(ROW_TILE, t)
    return pl.pallas_call(
        _qkv_rope_kernel,
        grid=(t // tm,),
        in_specs=[
            pl.BlockSpec((tm, d), lambda i: (i, 0)),
            pl.BlockSpec((d, n), lambda i: (0, 0)),
            pl.BlockSpec((tm, 1), lambda i: (i, 0)),
            pl.BlockSpec((1, LANES), lambda i: (0, 0)),
        ],
        out_specs=pl.BlockSpec((tm, n), lambda i: (i, 0)),
        out_shape=jax.ShapeDtypeStruct((t, n), BF16),
        compiler_params=_params("arbitrary"),
        name="qkv_rope",
    )(x2d, w_qkv, pos_col, inv_row)


def _diff_attn_kernel(q_ref, k_ref, v_ref, lam_ref, sw_ref, o_ref, qs_ref, m_ref, l_ref, acc_ref, bias_ref, *,
                      lambda_init):
    qi = pl.program_id(2)
    tq = q_ref.shape[0]
    tk = bias_ref.shape[2]

    @pl.when((pl.program_id(0) == 0) & (pl.program_id(1) == 0) & (qi == 0))
    def _():
        for d in range(tq // tk):
            qrow = lax.broadcasted_iota(jnp.int32, (2 * tq, tk), 0) % tq
            kcol = d * tk + lax.broadcasted_iota(jnp.int32, (2 * tq, tk), 1)
            bias_ref[d] = jnp.where(kcol <= qrow, 0.0, -jnp.inf)

    q = q_ref[...]
    lane = lax.broadcasted_iota(jnp.int32, q.shape, 1)
    zero = jnp.zeros_like(q)
    qs_ref[0:tq, :] = jnp.where(lane < DIFF_HD, q, zero)
    qs_ref[tq:2 * tq, :] = jnp.where(lane >= DIFF_HD, q, zero)
    m_ref[...] = jnp.full(m_ref.shape, -jnp.inf, F32)
    l_ref[...] = jnp.zeros_like(l_ref)
    acc_ref[...] = jnp.zeros_like(acc_ref)

    def block(kj, diag=None):
        k0 = pl.multiple_of(kj * tk, tk)
        s = _nt_dot(qs_ref[...], k_ref[pl.ds(k0, tk), :])
        if diag is not None:
            s = s + bias_ref[diag]
        m_old = m_ref[...]
        m_new = jnp.maximum(m_old, jnp.max(s, axis=-1, keepdims=True))
        p = jnp.exp2((s - jnp.concatenate([m_new] * (tk // LANES), axis=1)).astype(BF16))
        alpha = jnp.exp2(m_old - m_new)
        l_ref[...] = alpha * l_ref[...] + jnp.sum(p.astype(F32), axis=-1, keepdims=True)
        acc_ref[...] = alpha * acc_ref[...] + jnp.dot(p, v_ref[pl.ds(k0, tk), :], preferred_element_type=F32)
        m_ref[...] = m_new

    n_full = (qi * tq) // tk

    def group(size):
        def body(kg, c):
            for u in range(size):
                block(size * kg + u)
            return c
        return body

    done = 0
    for size in ATTN_GROUPS:
        trips = (n_full - done) // size
        lax.fori_loop(done // size, done // size + trips, group(size), 0)
        done = done + trips * size
    for d in range(tq // tk):
        block(n_full + d, d)

    lam = lam_ref[...]
    lam_full = (jnp.exp(jnp.sum(lam[0:1] * lam[1:2], axis=-1, keepdims=True))
                - jnp.exp(jnp.sum(lam[2:3] * lam[3:4], axis=-1, keepdims=True)) + lambda_init)
    o1 = acc_ref[0:tq, :] / l_ref[0:tq, :]
    o2 = acc_ref[tq:2 * tq, :] / l_ref[tq:2 * tq, :]
    o = o1 - lam_full * o2
    o = o * lax.rsqrt(jnp.mean(o * o, axis=-1, keepdims=True) + RMS_EPS) * sw_ref[...]
    o_ref[...] = (o * (1.0 - lambda_init)).astype(BF16)


def _diff_attn(qkv, lam, subln_w, b, s, lambda_init):
    t = b * s
    h, w = DIFF_HEADS, 2 * DIFF_HD
    tq = min(ATTN_Q, s)
    tk = min(ATTN_K, tq)
    assert tq % tk == 0
    nq = s // tq
    return pl.pallas_call(
        functools.partial(_diff_attn_kernel, lambda_init=lambda_init),
        grid=(b, h, nq),
        in_specs=[
            pl.BlockSpec((tq, w), lambda bi, hi, qi: (bi * nq + qi, hi)),
            pl.BlockSpec((s, w), lambda bi, hi, qi: (bi, h + hi)),
            pl.BlockSpec((s, DIFF_VD), lambda bi, hi, qi: (bi, 2 * h + hi)),
            pl.BlockSpec((4, DIFF_HD), lambda bi, hi, qi: (0, 0)),
            pl.BlockSpec((1, DIFF_VD), lambda bi, hi, qi: (0, 0)),
        ],
        out_specs=pl.BlockSpec((tq, DIFF_VD), lambda bi, hi, qi: (bi * nq + qi, hi)),
        out_shape=jax.ShapeDtypeStruct((t, h * DIFF_VD), BF16),
        scratch_shapes=[
            pltpu.VMEM((2 * tq, w), BF16),
            pltpu.VMEM((2 * tq, LANES), F32),
            pltpu.VMEM((2 * tq, LANES), F32),
            pltpu.VMEM((2 * tq, DIFF_VD), F32),
            pltpu.VMEM((tq // tk, 2 * tq, tk), F32),
        ],
        compiler_params=_params("arbitrary", "arbitrary", "arbitrary"),
        name="diff_attn",
    )(qkv, qkv, qkv, lam, subln_w)


def _pad_cols(a, n):
    return jnp.pad(a, ((0, 0), (0, n - a.shape[1])))


def kernel(x, positions, a_w_in, a_w_gk2, a_b_gk, a_norm_w, a_w_out, kv_w, b_w_q, b_lambda, b_subln_w,
           b_w_out, ln1_g, ln1_b, ln2_g, ln2_b, router_w, router_b, moe_w_gu, moe_b_gu, moe_w_down,
           moe_b_down):
    b, s, d = x.shape
    t = b * s
    xc = x.reshape(t, d)
    n_main = 2 * GLA_HEADS * GLA_DK + 2 * GLA_HEADS * GLA_DV
    half = DIFF_HD // 2
    inv = ROPE_THETA ** (-jnp.arange(half, dtype=F32) * 2.0 / DIFF_HD)
    inv_row = jnp.tile(inv, LANES // half).reshape(1, LANES)
    pos_col = positions.reshape(t, 1)
    assert DEPTH - N_A_LAYERS == 1
    for l in range(DEPTH):
        if l < N_A_LAYERS:
            w_in = a_w_in[l]
            qkvr, logg = _gla_proj(
                xc, w_in[:, :n_main].astype(BF16), _pad_cols(w_in[:, n_main:], LANES).astype(BF16),
                jnp.pad(a_w_gk2[l], ((0, LANES - GLA_GATE_RANK), (0, 0))), a_b_gk[l].reshape(1, -1))
            o = _gla_core(qkvr, logg, a_norm_w[l].reshape(1, -1), b, s)
            w_out = a_w_out[l]
        else:
            j = l - N_A_LAYERS
            lambda_init = 0.8 - 0.6 * math.exp(-0.3 * l)
            w_qkv = jnp.concatenate([b_w_q[j], kv_w], axis=1).astype(BF16)
            qkv = _qkv_rope(xc, w_qkv, pos_col, inv_row)
            o = _diff_attn(qkv, b_lambda[j], b_subln_w[j].reshape(1, -1), b, s, lambda_init)
            w_out = b_w_out[j]
        rw = _pad_cols(router_w[l], LANES)
        rw_hi = rw.astype(BF16)
        rw_lo = (rw - rw_hi.astype(F32)).astype(BF16)
        x1, logits = _proj_ln(o, w_out.astype(BF16), xc, ln1_g[l].reshape(1, -1), ln1_b[l].reshape(1, -1),
                              jnp.concatenate([rw_hi, rw_lo], axis=1), _pad_cols(router_b[l].reshape(1, -1), LANES))
        xc = _moe(x1, logits, ln2_g[l].reshape(1, -1), ln2_b[l].reshape(1, -1),
                  moe_w_gu, moe_b_gu, moe_w_down, moe_b_down, l)
    return xc.reshape(b, s, d)
```
